```python
import math
import jax
import jax.numpy as jnp
from jax import lax
import numpy as np

D_MODEL = 4096
BATCH = 4
SEQ = 4096
DEPTH = 1

CTX_LEN = 256
GRID_W = 64

RMS_EPS = 1e-6
N_MOD = 6

RWKV_WIDTH = D_MODEL // 2
RWKV_HEAD = 64
RWKV_HEADS = RWKV_WIDTH // RWKV_HEAD
DECAY_RANK = 96
ICL_RANK = 96
GATE_RANK = 256
RWKV_GN_EPS = 64e-5
RWKV_COLS = 3 * RWKV_WIDTH + 2 * DECAY_RANK + 2 * ICL_RANK + GATE_RANK
RWKV_SPLITS = (RWKV_WIDTH, 2 * RWKV_WIDTH, 3 * RWKV_WIDTH,
               3 * RWKV_WIDTH + 2 * DECAY_RANK,
               3 * RWKV_WIDTH + 2 * DECAY_RANK + 2 * ICL_RANK)

SSM_WIDTH = D_MODEL - RWKV_WIDTH
SSM_HEAD = 64
SSM_HEADS = SSM_WIDTH // SSM_HEAD
SSM_GROUPS = 8
SSM_STATE = 128
SSM_CONV = 3
SSM_CHUNK = 128
SSM_XBC = SSM_WIDTH + 2 * SSM_GROUPS * SSM_STATE
SSM_COLS = SSM_WIDTH + SSM_XBC + 2 * SSM_HEADS

IN_COLS = RWKV_COLS + SSM_COLS

PEER_HEADS = 8
PEER_KEYS = 128
PEER_EXPERTS = PEER_KEYS * PEER_KEYS
PEER_QDIM = 256
PEER_TOPK = 16
PEER_BLOCK = 64

kernel_name = "hybrid_rwkv7_ssd_peer_dit_block"


def rmsnorm(h, g):
    hf = h.astype(jnp.float32)
    hf = hf * lax.rsqrt(jnp.mean(hf * hf, axis=-1, keepdims=True) + RMS_EPS)
    return hf.astype(h.dtype) * g


def modulate(h, g, shift, scale):
    return rmsnorm(h, g) * (1 + scale) + shift


def stack_dirs(fwd, bwd):
    return jnp.concatenate([fwd, jnp.flip(bwd, axis=1)], axis=0)


def unstack_dirs(y):
    b = y.shape[0] // 2
    return y[:b] + jnp.flip(y[b:], axis=1)


def grid_token_shift(p, rows):
    b, t, ch = p.shape
    q = p.reshape(b, rows, GRID_W, ch // 4, 4)
    left = jnp.pad(q[:, :, :-1, :, 0], ((0, 0), (0, 0), (1, 0), (0, 0)))
    right = jnp.pad(q[:, :, 1:, :, 1], ((0, 0), (0, 0), (0, 1), (0, 0)))
    up = jnp.pad(q[:, :-1, :, :, 2], ((0, 0), (1, 0), (0, 0), (0, 0)))
    down = jnp.pad(q[:, 1:, :, :, 3], ((0, 0), (0, 1), (0, 0), (0, 0)))
    return jnp.stack([left, right, up, down], axis=-1).reshape(b, t, ch)


def seq_token_shift(p):
    b, t, ch = p.shape
    q = p.reshape(b, t, ch // 2, 2)
    prev = jnp.pad(q[:, :-1, :, 0], ((0, 0), (1, 0), (0, 0)))
    nxt = jnp.pad(q[:, 1:, :, 1], ((0, 0), (0, 1), (0, 0)))
    return jnp.stack([prev, nxt], axis=-1).reshape(b, t, ch)


def rwkv_prepare(cols, shifted, lp):
    b, t, _ = cols.shape
    xm = cols + lp["rwkv_mu"] * (shifted - cols)
    r, k, v, w_lr, a_lr, g_lr = jnp.split(xm, RWKV_SPLITS, axis=-1)
    w_lr = w_lr.reshape(b, t, 2, DECAY_RANK)
    a_lr = a_lr.reshape(b, t, 2, ICL_RANK)
    w = -jax.nn.softplus(-(lp["rwkv_w0"] + jnp.einsum("btdr,drc->btdc", jnp.tanh(w_lr), lp["rwkv_w2"]))) - 0.5
    decay = jnp.exp(-jnp.exp(w))
    a = jax.nn.sigmoid(lp["rwkv_a0"] + jnp.einsum("btdr,drc->btdc", a_lr, lp["rwkv_a2"]))
    g = jax.nn.sigmoid(g_lr) @ lp["rwkv_g2"]
    k_dir = k[:, :, None, :] * (1 + (a - 1) * lp["rwkv_k_a"])
    heads = lambda z: z.reshape(*z.shape[:-1], RWKV_HEADS, RWKV_HEAD)
    r_h, v_h, decay_h, a_h, k_h = heads(r), heads(v), heads(decay), heads(a), heads(k_dir)
    kk = heads(k * lp["rwkv_k_k"]).astype(jnp.float32)
    kk = (kk * lax.rsqrt(jnp.maximum(jnp.sum(kk * kk, axis=-1, keepdims=True), 1e-24))).astype(k.dtype)
    scan_in = (stack_dirs(r_h, r_h),
               stack_dirs(decay_h[:, :, 0], decay_h[:, :, 1]),
               stack_dirs(k_h[:, :, 0], k_h[:, :, 1]),
               stack_dirs(v_h, v_h),
               stack_dirs(-kk, -kk),
               stack_dirs(kk * a_h[:, :, 0], kk * a_h[:, :, 1]))
    bonus = jnp.einsum("bthn,btdhn,hn->bth", r_h, k_h, lp["rwkv_r_k"])[..., None] * v_h
    return scan_in, bonus, g


def rwkv7_scan(r, w, k, v, a, b, s0):
    def step(s, inp):
        r_t, w_t, k_t, v_t, a_t, b_t = inp
        sa = jnp.einsum("ghij,ghj->ghi", s, a_t)
        s = s * w_t[:, :, None, :] + sa[..., None] * b_t[:, :, None, :] + v_t[..., None] * k_t[:, :, None, :]
        return s, jnp.einsum("ghij,ghj->ghi", s, r_t)
    xs = tuple(jnp.moveaxis(z, 1, 0) for z in (r, w, k, v, a, b))
    s_fin, out = lax.scan(step, s0, xs)
    return jnp.moveaxis(out, 0, 1), s_fin


def rwkv_finish(out_dirs, bonus, g, lp):
    o = unstack_dirs(out_dirs).astype(jnp.float32)
    mean = jnp.mean(o, axis=-1, keepdims=True)
    var = jnp.mean(jnp.square(o - mean), axis=-1, keepdims=True)
    o = ((o - mean) * lax.rsqrt(var + RWKV_GN_EPS)).astype(g.dtype)
    b, t = o.shape[:2]
    o = o.reshape(b, t, RWKV_WIDTH) * lp["rwkv_ln_w"] + lp["rwkv_ln_b"]
    return (o + bonus.reshape(b, t, RWKV_WIDTH)) * g


def centred_dwconv(x, w, bias):
    half = (w.shape[0] - 1) // 2
    y = lax.conv_general_dilated(x, w[:, None, :], window_strides=(1,), padding=[(half, half)],
                                 dimension_numbers=("NWC", "WIO", "NWC"),
                                 feature_group_count=x.shape[-1])
    return y + bias


def ssd_prepare(cols, lp):
    b, t, _ = cols.shape
    z, xbc, dt_raw = jnp.split(cols, (SSM_WIDTH, SSM_WIDTH + SSM_XBC), axis=-1)
    xbc = jax.nn.silu(centred_dwconv(xbc, lp["ssm_conv_w"], lp["ssm_conv_b"]))
    xs, bm, cm = jnp.split(xbc, (SSM_WIDTH, SSM_WIDTH + SSM_GROUPS * SSM_STATE), axis=-1)
    xs = xs.reshape(b, t, SSM_HEADS, SSM_HEAD)
    bm = bm.reshape(b, t, SSM_GROUPS, SSM_STATE)
    cm = cm.reshape(b, t, SSM_GROUPS, SSM_STATE)
    dt = jax.nn.softplus(dt_raw.reshape(b, t, 2, SSM_HEADS) + lp["ssm_dt_bias"])
    log_a = dt * -jnp.exp(lp["ssm_a_log"])
    xdt = xs[:, :, None] * dt[..., None]
    scan_in = (stack_dirs(xdt[:, :, 0], xdt[:, :, 1]),
               stack_dirs(log_a[:, :, 0], log_a[:, :, 1]),
               stack_dirs(bm, bm),
               stack_dirs(cm, cm))
    return scan_in, xs, z


def ssd_chunked(x, la, bm, cm, s0):
    gb, t, h, p = x.shape
    ng, n = bm.shape[2], bm.shape[3]
    r = h // ng
    L = SSM_CHUNK
    nc = t // L
    x = x.reshape(gb, nc, L, ng, r, p)
    la = la.reshape(gb, nc, L, ng, r)
    bm = bm.reshape(gb, nc, L, ng, n)
    cm = cm.reshape(gb, nc, L, ng, n)
    cum = jnp.cumsum(la.astype(jnp.float32), axis=2)
    lower = jnp.tril(jnp.ones((L, L), dtype=bool))[None, None, :, :, None, None]
    seg = cum[:, :, :, None] - cum[:, :, None, :]
    decay_ls = jnp.exp(jnp.where(lower, seg, -jnp.inf)).astype(x.dtype)
    cb = jnp.einsum("bclgn,bcsgn->bclsg", cm, bm)
    y_diag = jnp.einsum("bclsg,bclsgr,bcsgrp->bclgrp", cb, decay_ls, x)
    decay_end = jnp.exp(cum[:, :, -1:] - cum).astype(x.dtype)
    chunk_states = jnp.einsum("bclgn,bclgr,bclgrp->bcgrpn", bm, decay_end, x)
    chunk_decay = jnp.exp(cum[:, :, -1]).astype(x.dtype)

    def carry_state(s, inp):
        dec, st = inp
        return dec[..., None, None] * s + st, s

    s_fin, s_prev = lax.scan(carry_state, s0.reshape(gb, ng, r, p, n),
                             (jnp.moveaxis(chunk_decay, 1, 0), jnp.moveaxis(chunk_states, 1, 0)))
    s_prev = jnp.moveaxis(s_prev, 0, 1)
    y_off = jnp.einsum("bclgn,bcgrpn,bclgr->bclgrp", cm, s_prev, jnp.exp(cum).astype(x.dtype))
    y = (y_diag + y_off).reshape(gb, t, h, p)
    return y, s_fin.reshape(gb, h, p, n)


def ssd_finish(y_dirs, xs, z, lp):
    b, t = xs.shape[:2]
    y = unstack_dirs(y_dirs) + lp["ssm_d"][:, None] * xs
    y = y.reshape(b, t, SSM_WIDTH) * jax.nn.silu(z)
    yf = y.reshape(b, t, SSM_GROUPS, SSM_WIDTH // SSM_GROUPS).astype(jnp.float32)
    yf = yf * lax.rsqrt(jnp.mean(yf * yf, axis=-1, keepdims=True) + RMS_EPS)
    return yf.reshape(b, t, SSM_WIDTH).astype(y.dtype) * lp["ssm_norm_w"]


def hybrid_mixer(n_x, n_c, rows, lp, ctx_out):
    b = n_x.shape[0]
    p_x = n_x @ lp["w_in"]
    p_c = n_c @ lp["w_in"]
    rw_x, ss_x = p_x[..., :RWKV_COLS], p_x[..., RWKV_COLS:]
    rw_c, ss_c = p_c[..., :RWKV_COLS], p_c[..., RWKV_COLS:]
    rin_c, rb_c, rg_c = rwkv_prepare(rw_c, seq_token_shift(rw_c), lp)
    rin_x, rb_x, rg_x = rwkv_prepare(rw_x, grid_token_shift(rw_x, rows), lp)
    s0 = jnp.zeros((2 * b, RWKV_HEADS, RWKV_HEAD, RWKV_HEAD), n_x.dtype)
    ro_c, rs_c = rwkv7_scan(*rin_c, s0)
    ro_x, _ = rwkv7_scan(*rin_x, rs_c)
    sin_c, xs_c, z_c = ssd_prepare(ss_c, lp)
    sin_x, xs_x, z_x = ssd_prepare(ss_x, lp)
    h0 = jnp.zeros((2 * b, SSM_HEADS, SSM_HEAD, SSM_STATE), n_x.dtype)
    so_c, hs_c = ssd_chunked(*sin_c, h0)
    so_x, _ = ssd_chunked(*sin_x, hs_c)
    y_x = jnp.concatenate([rwkv_finish(ro_x, rb_x, rg_x, lp), ssd_finish(so_x, xs_x, z_x, lp)], axis=-1) @ lp["w_out"]
    if ctx_out:
        y_c = jnp.concatenate([rwkv_finish(ro_c, rb_c, rg_c, lp), ssd_finish(so_c, xs_c, z_c, lp)], axis=-1) @ lp["w_out"]
    else:
        y_c = None
    return y_x, y_c


def peer_ffn(xn, lp):
    b, t, d = xn.shape
    w_q, k1, k2, u_tab, v_tab = lp["peer_wq"], lp["peer_k1"], lp["peer_k2"], lp["peer_u"], lp["peer_v"]

    def block(xb):
        q = (xb @ w_q).reshape(PEER_BLOCK, PEER_HEADS, 2, PEER_QDIM // 2)
        s1 = jnp.einsum("thd,hkd->thk", q[:, :, 0], k1).astype(jnp.float32)
        s2 = jnp.einsum("thd,hkd->thk", q[:, :, 1], k2).astype(jnp.float32)
        v1, i1 = lax.top_k(s1, PEER_TOPK)
        v2, i2 = lax.top_k(s2, PEER_TOPK)
        cand = (v1[..., :, None] + v2[..., None, :]).reshape(PEER_BLOCK, PEER_HEADS, PEER_TOPK * PEER_TOPK)
        best, flat = lax.top_k(cand, PEER_TOPK)
        expert = (jnp.take_along_axis(i1, flat // PEER_TOPK, axis=-1) * PEER_KEYS
                  + jnp.take_along_axis(i2, flat % PEER_TOPK, axis=-1))
        gate = jax.nn.softmax(best, axis=-1).astype(xb.dtype)
        act = jax.nn.gelu(jnp.einsum("td,thkd->thk", xb, u_tab[expert]), approximate=False)
        return jnp.einsum("thk,thkd->td", gate * act, v_tab[expert])

    out = lax.map(block, xn.reshape(-1, PEER_BLOCK, d))
    return out.reshape(b, t, d)


def setup_inputs(seed: int = 0) -> dict:
    key = jax.random.key(seed)
    keys = list(jax.random.split(key, 40))
    nxt = lambda: keys.pop()
    nrm = lambda shape, s: jax.random.normal(nxt(), shape, jnp.float32) * s
    uni = lambda shape, lo, hi: jax.random.uniform(nxt(), shape, jnp.float32, minval=lo, maxval=hi)
    L, C = DEPTH, RWKV_WIDTH
    dt_init = jnp.exp(uni((L, 2, SSM_HEADS), math.log(1e-3), math.log(1e-1)))
    return {
        "x": nrm((BATCH, SEQ, D_MODEL), 1.0),
        "c": nrm((BATCH, D_MODEL), 1.0),
        "ctx": nrm((BATCH, CTX_LEN, D_MODEL), 1.0),
        "c_ctx": nrm((D_MODEL,), 1.0),
        "w_mod": nrm((L, D_MODEL, N_MOD * D_MODEL), 0.5 * D_MODEL ** -0.5),
        "b_mod": nrm((L, N_MOD * D_MODEL), 0.01),
        "norm1_g": 1.0 + nrm((L, D_MODEL), 0.02),
        "norm2_g": 1.0 + nrm((L, D_MODEL), 0.02),
        "w_in": nrm((L, D_MODEL, IN_COLS), D_MODEL ** -0.5),
        "w_out": nrm((L, D_MODEL, D_MODEL), D_MODEL ** -0.5),
        "rwkv_mu": uni((L, RWKV_COLS), 0.0, 1.0),
        "rwkv_w0": uni((L, 2, C), -5.0, -1.0),
        "rwkv_w2": nrm((L, 2, DECAY_RANK, C), 0.5 * DECAY_RANK ** -0.5),
        "rwkv_a0": nrm((L, 2, C), 0.1),
        "rwkv_a2": nrm((L, 2, ICL_RANK, C), 0.5 * ICL_RANK ** -0.5),
        "rwkv_g2": nrm((L, GATE_RANK, C), GATE_RANK ** -0.5),
        "rwkv_k_k": 0.85 + nrm((L, C), 0.05),
        "rwkv_k_a": 1.0 + nrm((L, C), 0.05),
        "rwkv_r_k": nrm((L, RWKV_HEADS, RWKV_HEAD), 0.1),
        "rwkv_ln_w": 1.0 + nrm((L, C), 0.02),
        "rwkv_ln_b": nrm((L, C), 0.01),
        "ssm_conv_w": nrm((L, SSM_CONV, SSM_XBC), SSM_CONV ** -0.5),
        "ssm_conv_b": nrm((L, SSM_XBC), 0.01),
        "ssm_dt_bias": dt_init + jnp.log(-jnp.expm1(-dt_init)),
        "ssm_a_log": jnp.log(uni((L, 2, SSM_HEADS), 1.0, 16.0)),
        "ssm_d": 1.0 + nrm((L, SSM_HEADS), 0.1),
        "ssm_norm_w": 1.0 + nrm((L, SSM_WIDTH), 0.02),
        "peer_wq": nrm((L, D_MODEL, PEER_HEADS * PEER_QDIM), D_MODEL ** -0.5),
        "peer_k1": nrm((L, PEER_HEADS, PEER_KEYS, PEER_QDIM // 2), (PEER_QDIM // 2) ** -0.5),
        "peer_k2": nrm((L, PEER_HEADS, PEER_KEYS, PEER_QDIM // 2), (PEER_QDIM // 2) ** -0.5),
        "peer_u": nrm((L, PEER_EXPERTS, D_MODEL), D_MODEL ** -0.5),
        "peer_v": nrm((L, PEER_EXPERTS, D_MODEL), 0.5),
        "final_g": 1.0 + nrm((D_MODEL,), 0.02),
    }


def reference(x, c, ctx, c_ctx, w_mod, b_mod, norm1_g, norm2_g, w_in, w_out,
              rwkv_mu, rwkv_w0, rwkv_w2, rwkv_a0, rwkv_a2, rwkv_g2, rwkv_k_k, rwkv_k_a,
              rwkv_r_k, rwkv_ln_w, rwkv_ln_b, ssm_conv_w, ssm_conv_b, ssm_dt_bias, ssm_a_log,
              ssm_d, ssm_norm_w, peer_wq, peer_k1, peer_k2, peer_u, peer_v, final_g):
    rows = x.shape[1] // GRID_W
    h_x, h_c = x, ctx
    s_c = jax.nn.silu(c)
    s_cc = jax.nn.silu(c_ctx)
    for l in range(DEPTH):
        last = l == DEPTH - 1
        lp = {"w_in": w_in[l], "w_out": w_out[l],
              "rwkv_mu": rwkv_mu[l], "rwkv_w0": rwkv_w0[l], "rwkv_w2": rwkv_w2[l],
              "rwkv_a0": rwkv_a0[l], "rwkv_a2": rwkv_a2[l], "rwkv_g2": rwkv_g2[l],
              "rwkv_k_k": rwkv_k_k[l], "rwkv_k_a": rwkv_k_a[l], "rwkv_r_k": rwkv_r_k[l],
              "rwkv_ln_w": rwkv_ln_w[l], "rwkv_ln_b": rwkv_ln_b[l],
              "ssm_conv_w": ssm_conv_w[l], "ssm_conv_b": ssm_conv_b[l],
              "ssm_dt_bias": ssm_dt_bias[l], "ssm_a_log": ssm_a_log[l],
              "ssm_d": ssm_d[l], "ssm_norm_w": ssm_norm_w[l],
              "peer_wq": peer_wq[l], "peer_k1": peer_k1[l], "peer_k2": peer_k2[l],
              "peer_u": peer_u[l], "peer_v": peer_v[l]}
        mod_x = (s_c @ w_mod[l] + b_mod[l])[:, None, :]
        mod_c = (s_cc @ w_mod[l] + b_mod[l])[None, None, :]
        sh1x, sc1x, gt1x, sh2x, sc2x, gt2x = jnp.split(mod_x, N_MOD, axis=-1)
        sh1c, sc1c, gt1c, sh2c, sc2c, gt2c = jnp.split(mod_c, N_MOD, axis=-1)
        n_x = modulate(h_x, norm1_g[l], sh1x, sc1x)
        n_c = modulate(h_c, norm1_g[l], sh1c, sc1c)
        o_x, o_c = hybrid_mixer(n_x, n_c, rows, lp, not last)
        h_x = h_x + gt1x * o_x
        h_x = h_x + gt2x * peer_ffn(modulate(h_x, norm2_g[l], sh2x, sc2x), lp)
        if not last:
            h_c = h_c + gt1c * o_c
            h_c = h_c + gt2c * peer_ffn(modulate(h_c, norm2_g[l], sh2c, sc2c), lp)
    return rmsnorm(h_x, final_g)
```

```python
import functools
import math

import jax
import jax.numpy as jnp
from jax import lax
from jax.experimental import pallas as pl
from jax.experimental.pallas import tpu as pltpu

F32 = jnp.float32
BF16 = jnp.bfloat16
HI = lax.Precision.HIGHEST

RWKV_CHUNK = 64
RWKV_HEAD = 64
PAIR = 2 * RWKV_HEAD


def _mm(a, b, precision=None):
    return lax.dot_general(a, b, (((1,), (0,)), ((), ())), precision=precision,
                           preferred_element_type=F32)


def _mm_nt(a, b, precision=None):
    return lax.dot_general(a, b, (((1,), (1,)), ((), ())), precision=precision,
                           preferred_element_type=F32)


def _rwkv_masks(is_bwd):
    rho = lax.broadcasted_iota(jnp.int32, (PAIR, PAIR), 0)
    sig = lax.broadcasted_iota(jnp.int32, (PAIR, PAIR), 1)
    same64 = (rho // 64) == (sig // 64)
    same32 = (rho // 32) == (sig // 32)
    same16 = (rho // 16) == (sig // 16)
    sgn = jnp.where(is_bwd, -1, 1).astype(jnp.int32)
    before = ((rho - sig) * sgn > 0) & same64
    eye = rho == sig
    t = lax.broadcasted_iota(jnp.int32, (RWKV_CHUNK, RWKV_CHUNK), 0)
    s = lax.broadcasted_iota(jnp.int32, (RWKV_CHUNK, RWKV_CHUNK), 1)
    tri = jnp.where((t - s) * sgn >= 0, 1.0, 0.0).astype(F32)
    return dict(same64=same64, d16=same16, off32=same32 & ~same16, off64=same64 & ~same32,
                before=before, before_eq=before | eye, eye=eye, tri=tri)


def _rwkv_pair_chunk(r, v, kk, ld, kd, bb, S, m):
    zero = jnp.zeros((), F32)
    cum = _mm(m["tri"], ld, HI)
    tot = jnp.sum(ld, axis=0, keepdims=True)
    e_cum = jnp.exp(cum)
    e_inv = jnp.exp(-cum)
    e_end = jnp.exp(tot - cum)
    a_t = -kk * jnp.exp(cum - ld)
    r_t = r * e_cum
    b_h = bb * e_inv
    k_h = kd * e_inv
    b_g = bb * e_end
    k_g = kd * e_end

    bd = m["same64"]

    def stack2(x):
        return jnp.where(bd, jnp.concatenate([x, x], axis=0), zero)

    a2, r2, b2, k2, bg2, kg2, v2 = (stack2(z) for z in (a_t, r_t, b_h, k_h, b_g, k_g, v))
    sc = _mm_nt(jnp.concatenate([a2, r2], axis=0), jnp.concatenate([b2, k2], axis=0), HI)
    mm = jnp.where(m["before"], sc[:PAIR, :PAIR], zero)
    nn = jnp.where(m["before"], sc[:PAIR, PAIR:], zero)
    pp = jnp.where(m["before_eq"], sc[PAIR:, :PAIR], zero)
    qq = jnp.where(m["before_eq"], sc[PAIR:, PAIR:], zero)

    md = jnp.where(m["d16"], mm, zero)
    m2 = _mm(md, md, HI)
    m4 = _mm(m2, m2, HI)
    m8 = _mm(m4, m4, HI)
    t_inv = jnp.where(m["eye"], 1.0, zero) + md
    t_inv = t_inv + _mm(t_inv, m2, HI)
    t_inv = t_inv + _mm(t_inv, m4, HI)
    t_inv = t_inv + _mm(t_inv, m8, HI)
    for off in ("off32", "off64"):
        mo = jnp.where(m[off], mm, zero)
        t_inv = t_inv + _mm(t_inv, _mm(mo, t_inv, HI), HI)

    at = a2.T
    vt = v2.T
    rt = r2.T
    atp = _mm_nt(at, t_inv, HI)
    c1 = _mm_nt(_mm_nt(vt, nn, HI), t_inv, HI)
    g = _mm(atp, bg2, HI) + jnp.where(m["eye"], jnp.exp(tot), zero)
    c2 = _mm(c1, bg2, HI) + _mm(vt, kg2, HI)
    rtp = rt + _mm_nt(atp, pp, HI)
    c3 = _mm_nt(c1, pp, HI) + _mm_nt(vt, qq, HI)
    o_t = _mm(S, rtp, HI) + c3
    s_new = _mm(S, g, HI) + c2
    o2 = o_t.T
    return o2[:RWKV_CHUNK] + o2[RWKV_CHUNK:], s_new


def _rwkv_kernel(r_ref, v_ref, kk_ref, ld_ref, kd_ref, bb_ref, o_ref, s_ref, *, npair, nch):
    is_bwd = pl.program_id(0) == 1
    blk = pl.program_id(3)

    @pl.when(blk == 0)
    def _():
        s_ref[...] = jnp.zeros_like(s_ref)

    m = _rwkv_masks(is_bwd)

    def chunk(c, carry):
        ci = jnp.where(is_bwd, nch - 1 - c, c)
        rows = pl.ds(pl.multiple_of(ci * RWKV_CHUNK, RWKV_CHUNK), RWKV_CHUNK)
        for p in range(npair):
            lanes = slice(p * PAIR, (p + 1) * PAIR)
            o, s_new = _rwkv_pair_chunk(
                r_ref[0, rows, lanes], v_ref[0, rows, lanes], kk_ref[0, rows, lanes],
                ld_ref[0, 0, rows, lanes], kd_ref[0, 0, rows, lanes], bb_ref[0, 0, rows, lanes],
                s_ref[p], m)
            s_ref[p] = s_new
            o_ref[0, 0, rows, lanes] = o
        return carry

    lax.fori_loop(0, nch, chunk, 0)


def rwkv_scan(r, v, kk, ld, kd, bb, *, ctx_len, tb=256, lanes=512, interpret=False):
    b, tt, c = r.shape
    assert ctx_len % tb == 0 and tt % tb == 0 and c % lanes == 0 and lanes % PAIR == 0
    nctx = ctx_len // tb
    nblk = tt // tb

    def tmap(d, k):
        fwd = k
        bwd = jnp.where(k < nctx, nctx - 1 - k, nblk - 1 - (k - nctx))
        return jnp.where(d == 1, bwd, fwd)

    shared = pl.BlockSpec((1, tb, lanes), lambda d, bi, h, k: (bi, tmap(d, k), h))
    direc = pl.BlockSpec((1, 1, tb, lanes), lambda d, bi, h, k: (d, bi, tmap(d, k), h))
    out = pl.BlockSpec((1, 1, tb, lanes),
                       lambda d, bi, h, k: (d, bi, tmap(d, jnp.maximum(k, nctx)) - nctx, h))
    npair = lanes // PAIR
    return pl.pallas_call(
        functools.partial(_rwkv_kernel, npair=npair, nch=tb // RWKV_CHUNK),
        grid=(2, b, c // lanes, nblk),
        in_specs=[shared, shared, shared, direc, direc, direc],
        out_specs=out,
        out_shape=jax.ShapeDtypeStruct((2, b, tt - ctx_len, c), F32),
        scratch_shapes=[pltpu.VMEM((npair, PAIR, PAIR), F32)],
        compiler_params=pltpu.CompilerParams(
            dimension_semantics=("arbitrary", "arbitrary", "arbitrary", "arbitrary")),
        name="rwkv_scan",
        interpret=interpret,
    )(r, v, kk, ld, kd, bb)


def _time_block_map(nctx, nblk):
    def tmap(d, k):
        bwd = jnp.where(k < nctx, nctx - 1 - k, nblk - 1 - (k - nctx))
        return jnp.where(d == 1, bwd, k)
    return tmap


SSD_P = 64


def _ssd_kernel(xs_ref, bm_ref, cm_ref, dtc_ref, lac_ref, dtr_ref, lar_ref, y_ref, st_ref, *, chunk):
    is_bwd = pl.program_id(0) == 1
    blk = pl.program_id(3)

    @pl.when(blk == 0)
    def _():
        st_ref[...] = jnp.zeros_like(st_ref)

    sgn = jnp.where(is_bwd, -1, 1).astype(jnp.int32)
    l_i = lax.broadcasted_iota(jnp.int32, (chunk, chunk), 0)
    s_i = lax.broadcasted_iota(jnp.int32, (chunk, chunk), 1)
    incl = (l_i - s_i) * sgn >= 0
    tri = jnp.where(incl, 1.0, 0.0).astype(F32)
    lane = lax.broadcasted_iota(jnp.int32, (chunk, PAIR), 1)
    first = lane < SSD_P
    first_row = first[:1]

    bmat = bm_ref[0]
    cmat = cm_ref[0]
    dtc = dtc_ref[0, 0, 0]
    lac = lac_ref[0, 0, 0]
    dtr = dtr_ref[0, 0, 0]
    lar = lar_ref[0, 0, 0]
    nh = dtc.shape[-1]

    cb = _mm_nt(cmat, bmat, HI)
    b_t = bmat.T
    cum_col = _mm(tri, lac, HI)
    cum_row = _mm_nt(lar, tri, HI)
    tot = jnp.sum(lac, axis=0, keepdims=True)
    coef = jnp.exp(tot - cum_col) * dtc
    ecum = jnp.exp(cum_col)
    etot = jnp.exp(tot)

    for q in range(nh // 2):
        lanes = slice(q * PAIR, (q + 1) * PAIR)
        h0, h1 = 2 * q, 2 * q + 1
        xp = xs_ref[0, :, lanes]
        st = st_ref[:, lanes]
        ys = []
        for h in (h0, h1):
            seg = cum_col[:, h:h + 1] - cum_row[h:h + 1, :]
            dec = jnp.exp(jnp.where(incl, seg, -jnp.inf))
            ys.append(_mm(cb * dec * dtr[h:h + 1, :], xp, HI))
        y = jnp.where(first, ys[0], ys[1])
        y = y + _mm(cmat, st, HI) * jnp.where(first, ecum[:, h0:h0 + 1], ecum[:, h1:h1 + 1])
        xc = xp * jnp.where(first, coef[:, h0:h0 + 1], coef[:, h1:h1 + 1])
        st_ref[:, lanes] = (st * jnp.where(first_row, etot[:, h0:h0 + 1], etot[:, h1:h1 + 1])
                            + _mm(b_t, xc, HI))
        y_ref[0, 0, :, lanes] = y


def ssd_scan(xs, bm, cm, dtc, lac, dtr, lar, *, ctx_len, chunk=256, interpret=False):
    b, tt, width = xs.shape
    ng, nh = dtc.shape[2], dtc.shape[4]
    nst = bm.shape[-1] // ng
    gw = nh * SSD_P
    assert width == ng * gw and ctx_len % chunk == 0 and tt % chunk == 0 and nh % 2 == 0
    nctx, nblk = ctx_len // chunk, tt // chunk
    tmap = _time_block_map(nctx, nblk)
    return pl.pallas_call(
        functools.partial(_ssd_kernel, chunk=chunk),
        grid=(2, b, ng, nblk),
        in_specs=[
            pl.BlockSpec((1, chunk, gw), lambda d, bi, g, k: (bi, tmap(d, k), g)),
            pl.BlockSpec((1, chunk, nst), lambda d, bi, g, k: (bi, tmap(d, k), g)),
            pl.BlockSpec((1, chunk, nst), lambda d, bi, g, k: (bi, tmap(d, k), g)),
            pl.BlockSpec((1, 1, 1, chunk, nh), lambda d, bi, g, k: (d, bi, g, tmap(d, k), 0)),
            pl.BlockSpec((1, 1, 1, chunk, nh), lambda d, bi, g, k: (d, bi, g, tmap(d, k), 0)),
            pl.BlockSpec((1, 1, 1, nh, chunk), lambda d, bi, g, k: (d, bi, g, 0, tmap(d, k))),
            pl.BlockSpec((1, 1, 1, nh, chunk), lambda d, bi, g, k: (d, bi, g, 0, tmap(d, k))),
        ],
        out_specs=pl.BlockSpec(
            (1, 1, chunk, gw),
            lambda d, bi, g, k: (d, bi, tmap(d, jnp.maximum(k, nctx)) - nctx, g)),
        out_shape=jax.ShapeDtypeStruct((2, b, tt - ctx_len, width), F32),
        scratch_shapes=[pltpu.VMEM((nst, gw), F32)],
        compiler_params=pltpu.CompilerParams(
            dimension_semantics=("arbitrary", "arbitrary", "arbitrary", "arbitrary")),
        name="ssd_scan",
        interpret=interpret,
    )(xs, bm, cm, dtc, lac, dtr, lar)


RMS_EPS = 1e-6
VMEM_LIMIT = 56 * 1024 * 1024


def _mod_kernel(s_ref, w_ref, b_ref, o_ref):
    s = s_ref[...]
    s = s * jax.nn.sigmoid(s)
    o_ref[...] = _mm(s, w_ref[...], HI) + b_ref[...]


def mod_matmul(s, w, bias, *, tn=512, interpret=False):
    m, k = s.shape
    n = w.shape[1]
    return pl.pallas_call(
        _mod_kernel,
        grid=(n // tn,),
        in_specs=[pl.BlockSpec((m, k), lambda j: (0, 0)),
                  pl.BlockSpec((k, tn), lambda j: (0, j)),
                  pl.BlockSpec((1, tn), lambda j: (0, j))],
        out_specs=pl.BlockSpec((m, tn), lambda j: (0, j)),
        out_shape=jax.ShapeDtypeStruct((m, n), F32),
        compiler_params=pltpu.CompilerParams(dimension_semantics=("arbitrary",),
                                             vmem_limit_bytes=VMEM_LIMIT),
        name="mod_matmul",
        interpret=interpret,
    )(s, w, bias.reshape(1, n))


def _normmod_matmul_kernel(h_ref, g_ref, sh_ref, sc_ref, w_ref, o_ref, *rest, emit_xn):
    xn_ref = rest[-1]

    @pl.when(pl.program_id(1) == 0)
    def _():
        h = h_ref[...]
        hn = h * lax.rsqrt(jnp.mean(h * h, axis=-1, keepdims=True) + RMS_EPS)
        xn = (hn * g_ref[...]) * (1.0 + sc_ref[0]) + sh_ref[0]
        xn_ref[...] = xn.astype(BF16)
        if emit_xn:
            rest[0][...] = xn.astype(BF16)

    o_ref[...] = _mm(xn_ref[...], w_ref[...])


def normmod_matmul(h, g, mod3, w, *, row_of_block, shift_idx, scale_idx, n_mod, tm, tn,
                   emit_xn=False, interpret=False):
    rows, d = h.shape
    n = w.shape[1]
    assert rows % tm == 0 and n % tn == 0
    out_shape = [jax.ShapeDtypeStruct((rows, n), F32)]
    out_specs = [pl.BlockSpec((tm, tn), lambda i, j: (i, j))]
    if emit_xn:
        out_shape.append(jax.ShapeDtypeStruct((rows, d), BF16))
        out_specs.append(pl.BlockSpec((tm, d), lambda i, j: (i, 0)))
    res = pl.pallas_call(
        functools.partial(_normmod_matmul_kernel, emit_xn=emit_xn),
        grid=(rows // tm, n // tn),
        in_specs=[
            pl.BlockSpec((tm, d), lambda i, j: (i, 0)),
            pl.BlockSpec((1, d), lambda i, j: (0, 0)),
            pl.BlockSpec((1, 1, d), lambda i, j: (row_of_block(i) * n_mod + shift_idx, 0, 0)),
            pl.BlockSpec((1, 1, d), lambda i, j: (row_of_block(i) * n_mod + scale_idx, 0, 0)),
            pl.BlockSpec((d, tn), lambda i, j: (0, j)),
        ],
        out_specs=out_specs,
        out_shape=out_shape,
        scratch_shapes=[pltpu.VMEM((tm, d), BF16)],
        compiler_params=pltpu.CompilerParams(dimension_semantics=("arbitrary", "arbitrary"),
                                             vmem_limit_bytes=VMEM_LIMIT),
        name="normmod_matmul",
        interpret=interpret,
    )(h, g.reshape(1, d), mod3, mod3, w)
    return res if emit_xn else res[0]


def _matmul_gated_resid_kernel(y_ref, w_ref, x_ref, gt_ref, o_ref):
    o_ref[...] = x_ref[...] + gt_ref[0] * _mm(y_ref[...], w_ref[...])


def matmul_gated_resid(y, w, resid, mod3, *, row_of_block, gate_idx, n_mod, tm, tn, interpret=False):
    rows, k = y.shape
    n = w.shape[1]
    assert rows % tm == 0 and n % tn == 0
    return pl.pallas_call(
        _matmul_gated_resid_kernel,
        grid=(rows // tm, n // tn),
        in_specs=[
            pl.BlockSpec((tm, k), lambda i, j: (i, 0)),
            pl.BlockSpec((k, tn), lambda i, j: (0, j)),
            pl.BlockSpec((tm, tn), lambda i, j: (i, j)),
            pl.BlockSpec((1, 1, tn), lambda i, j: (row_of_block(i) * n_mod + gate_idx, 0, j)),
        ],
        out_specs=pl.BlockSpec((tm, tn), lambda i, j: (i, j)),
        out_shape=jax.ShapeDtypeStruct((rows, n), F32),
        compiler_params=pltpu.CompilerParams(dimension_semantics=("arbitrary", "arbitrary"),
                                             vmem_limit_bytes=VMEM_LIMIT),
        name="matmul_gated_resid",
        interpret=interpret,
    )(y, w, resid, mod3)


def _top_values(s, k):
    n = s.shape[0]
    row = lax.broadcasted_iota(jnp.int32, s.shape, 0)
    vals = []
    for _ in range(k):
        mx = jnp.max(s, axis=0, keepdims=True)
        first = jnp.min(jnp.where(s == mx, row, n), axis=0, keepdims=True)
        s = jnp.where(row == first, -jnp.inf, s)
        vals.append(mx)
    return jnp.concatenate(vals, axis=0)


def _peer_score_kernel(q_ref, k1_ref, k2_ref, s1_ref, s2_ref, e1_ref, e2_ref, tau_ref, *, topk):
    nheads, nkeys, qd = k1_ref.shape
    taus = []
    for h in range(nheads):
        q1 = q_ref[:, h * 2 * qd:(h * 2 + 1) * qd]
        q2 = q_ref[:, (h * 2 + 1) * qd:(h * 2 + 2) * qd]
        s1 = _mm_nt(k1_ref[h], q1, HI)
        s2 = _mm_nt(k2_ref[h], q2, HI)
        v1 = _top_values(s1, topk)
        v2 = _top_values(s2, topk)
        cand = jnp.concatenate([v1[a:a + 1] + v2 for a in range(topk)], axis=0)
        best = _top_values(cand, topk)
        z = jnp.sum(jnp.exp(best - best[:1]), axis=0, keepdims=True)
        s1_ref[h] = s1
        s2_ref[h] = s2
        e1_ref[h] = jnp.exp(s1 - v1[:1])
        e2_ref[h] = jnp.exp(s2 - v2[:1]) / z
        taus.append(best[topk - 1:topk])
    tau_ref[...] = jnp.concatenate(taus, axis=0)


def peer_scores(q, k1, k2, *, topk, tb=256, interpret=False):
    t = q.shape[0]
    nheads, nkeys, qd = k1.shape
    big = pl.BlockSpec((nheads, nkeys, tb), lambda i: (0, 0, i))
    big_shape = jax.ShapeDtypeStruct((nheads, nkeys, t), F32)
    return pl.pallas_call(
        functools.partial(_peer_score_kernel, topk=topk),
        grid=(t // tb,),
        in_specs=[pl.BlockSpec((tb, q.shape[1]), lambda i: (i, 0)),
                  pl.BlockSpec(k1.shape, lambda i: (0, 0, 0)),
                  pl.BlockSpec(k2.shape, lambda i: (0, 0, 0))],
        out_specs=[big, big, big, big, pl.BlockSpec((nheads, tb), lambda i: (0, i))],
        out_shape=[big_shape] * 4 + [jax.ShapeDtypeStruct((nheads, t), F32)],
        compiler_params=pltpu.CompilerParams(dimension_semantics=("arbitrary",),
                                             vmem_limit_bytes=VMEM_LIMIT),
        name="peer_scores",
        interpret=interpret,
    )(q, k1, k2)


def _peer_weight_kernel(u_ref, x_ref, s1_ref, s2_ref, e1_ref, e2_ref, tau_ref, w_ref):
    nheads, nkeys, _ = s2_ref.shape
    a = _mm_nt(u_ref[...], x_ref[...])
    act = 0.5 * a * (1.0 + lax.erf(a * (1.0 / math.sqrt(2.0))))
    for ii in range(u_ref.shape[0] // nkeys):
        gate = jnp.zeros((nkeys, x_ref.shape[0]), F32)
        for h in range(nheads):
            pair = s1_ref[h, ii:ii + 1, :] + s2_ref[h]
            gate = gate + jnp.where(pair >= tau_ref[h:h + 1, :],
                                    e1_ref[h, ii:ii + 1, :] * e2_ref[h], 0.0)
        rows = slice(ii * nkeys, (ii + 1) * nkeys)
        w_ref[rows, :] = (act[rows] * gate).astype(w_ref.dtype)


def peer_weights(u, xn, s1, s2, e1, e2, tau, *, te=1024, tb=512, interpret=False):
    ne, d = u.shape
    t = xn.shape[0]
    nheads, nkeys, _ = s1.shape
    ni = te // nkeys
    s1_spec = pl.BlockSpec((nheads, ni, tb), lambda i, j: (0, j, i))
    s2_spec = pl.BlockSpec((nheads, nkeys, tb), lambda i, j: (0, 0, i))
    return pl.pallas_call(
        _peer_weight_kernel,
        grid=(t // tb, ne // te),
        in_specs=[pl.BlockSpec((te, d), lambda i, j: (j, 0)),
                  pl.BlockSpec((tb, d), lambda i, j: (i, 0)),
                  s1_spec, s2_spec, s1_spec, s2_spec,
                  pl.BlockSpec((nheads, tb), lambda i, j: (0, i))],
        out_specs=pl.BlockSpec((te, tb), lambda i, j: (j, i)),
        out_shape=jax.ShapeDtypeStruct((ne, t), BF16),
        compiler_params=pltpu.CompilerParams(dimension_semantics=("arbitrary", "arbitrary"),
                                             vmem_limit_bytes=VMEM_LIMIT),
        name="peer_weights",
        interpret=interpret,
    )(u, xn, s1, s2, e1, e2, tau)


def _peer_out_kernel(w_ref, v_ref, o_ref):
    @pl.when(pl.program_id(1) == 0)
    def _():
        o_ref[...] = jnp.zeros_like(o_ref)

    o_ref[...] += lax.dot_general(w_ref[...], v_ref[...], (((0,), (0,)), ((), ())),
                                  preferred_element_type=F32)


def peer_out(wt, v, *, tb=512, te=512, interpret=False):
    ne, t = wt.shape
    d = v.shape[1]
    return pl.pallas_call(
        _peer_out_kernel,
        grid=(t // tb, ne // te),
        in_specs=[pl.BlockSpec((te, tb), lambda i, j: (j, i)),
                  pl.BlockSpec((te, d), lambda i, j: (j, 0))],
        out_specs=pl.BlockSpec((tb, d), lambda i, j: (i, 0)),
        out_shape=jax.ShapeDtypeStruct((t, d), F32),
        compiler_params=pltpu.CompilerParams(dimension_semantics=("arbitrary", "arbitrary"),
                                             vmem_limit_bytes=VMEM_LIMIT),
        name="peer_out",
        interpret=interpret,
    )(wt, v)


def _final_kernel(h_ref, p_ref, gt_ref, g_ref, o_ref):
    h = h_ref[...] + gt_ref[0] * p_ref[...]
    o_ref[...] = h * lax.rsqrt(jnp.mean(h * h, axis=-1, keepdims=True) + RMS_EPS) * g_ref[...]


def final_norm(h, p, mod3, g, *, row_of_block, gate_idx, n_mod, tm=256, interpret=False):
    rows, d = h.shape
    blk = pl.BlockSpec((tm, d), lambda i: (i, 0))
    return pl.pallas_call(
        _final_kernel,
        grid=(rows // tm,),
        in_specs=[blk, blk,
                  pl.BlockSpec((1, 1, d), lambda i: (row_of_block(i) * n_mod + gate_idx, 0, 0)),
                  pl.BlockSpec((1, d), lambda i: (0, 0))],
        out_specs=blk,
        out_shape=jax.ShapeDtypeStruct((rows, d), F32),
        compiler_params=pltpu.CompilerParams(dimension_semantics=("arbitrary",),
                                             vmem_limit_bytes=VMEM_LIMIT),
        name="final_norm",
        interpret=interpret,
    )(h, p, mod3, g.reshape(1, d))


N_MOD = 6
GRID_W = 64
RWKV_GN_EPS = 64e-5
SSM_STATE = 128
PEER_TOPK = 16


def _grid_token_shift(p, rows):
    b, t, ch = p.shape
    q = p.reshape(b, rows, t // rows, ch // 4, 4)
    z = ((0, 0),)
    left = jnp.pad(q[:, :, :-1, :, 0], z * 2 + ((1, 0),) + z)
    right = jnp.pad(q[:, :, 1:, :, 1], z * 2 + ((0, 1),) + z)
    up = jnp.pad(q[:, :-1, :, :, 2], z + ((1, 0),) + z * 2)
    down = jnp.pad(q[:, 1:, :, :, 3], z + ((0, 1),) + z * 2)
    return jnp.stack([left, right, up, down], axis=-1).reshape(b, t, ch)


def _seq_token_shift(p):
    b, t, ch = p.shape
    q = p.reshape(b, t, ch // 2, 2)
    prev = jnp.pad(q[:, :-1, :, 0], ((0, 0), (1, 0), (0, 0)))
    nxt = jnp.pad(q[:, 1:, :, 1], ((0, 0), (0, 1), (0, 0)))
    return jnp.stack([prev, nxt], axis=-1).reshape(b, t, ch)


def _dirs_first(z):
    return jnp.moveaxis(z, 2, 0)


def _rwkv_prepare(cols, shifted, cl, mu, w0, w2, a0, a2, g2, k_k, k_a, r_k):
    b, t, _ = cols.shape
    cw = k_k.shape[0]
    nh = cw // RWKV_HEAD
    dr, ir = w2.shape[1], a2.shape[1]
    xm = cols + mu * (shifted - cols)
    r, k, v, w_lr, a_lr, g_lr = jnp.split(
        xm, (cw, 2 * cw, 3 * cw, 3 * cw + 2 * dr, 3 * cw + 2 * dr + 2 * ir), axis=-1)
    w_lr = w_lr.reshape(b, t, 2, dr)
    a_lr = a_lr.reshape(b, t, 2, ir)
    w = -jax.nn.softplus(-(w0 + jnp.einsum("btdr,drc->btdc", jnp.tanh(w_lr), w2))) - 0.5
    ld = -jnp.exp(w)
    a = jax.nn.sigmoid(a0 + jnp.einsum("btdr,drc->btdc", a_lr, a2))
    k_dir = k[:, :, None, :] * (1 + (a - 1) * k_a)
    heads = lambda z: z.reshape(*z.shape[:-1], nh, RWKV_HEAD)
    kk = heads(k * k_k)
    kk = (kk * lax.rsqrt(jnp.maximum(jnp.sum(kk * kk, axis=-1, keepdims=True), 1e-24))).reshape(b, t, cw)
    bb = kk[:, :, None, :] * a
    rx, vx, kx = heads(r[:, cl:]), heads(v[:, cl:]), heads(k_dir[:, cl:])
    bonus = (jnp.einsum("bthn,btdhn,hn->bth", rx, kx, r_k)[..., None] * vx).reshape(b, t - cl, cw)
    gate = jax.nn.sigmoid(g_lr[:, cl:]) @ g2
    return r, v, kk, _dirs_first(ld), _dirs_first(k_dir), _dirs_first(bb), bonus, gate


def _rwkv_finish(o, bonus, gate, ln_w, ln_b):
    b, t, cw = o.shape
    oh = o.reshape(b, t, cw // RWKV_HEAD, RWKV_HEAD)
    mean = jnp.mean(oh, axis=-1, keepdims=True)
    var = jnp.mean(jnp.square(oh - mean), axis=-1, keepdims=True)
    oh = (oh - mean) * lax.rsqrt(var + RWKV_GN_EPS)
    return (oh.reshape(b, t, cw) * ln_w + ln_b + bonus) * gate


def _centred_dwconv3(xv, w, bias):
    prev = jnp.pad(xv[:, :-1], ((0, 0), (1, 0), (0, 0)))
    nxt = jnp.pad(xv[:, 1:], ((0, 0), (0, 1), (0, 0)))
    return w[0] * prev + w[1] * xv + w[2] * nxt + bias


def _ssd_prepare(ss_c, ss_x, sw, conv_w, conv_b, dt_bias, a_log):
    assert conv_w.shape[0] == 3
    xbc_w = conv_w.shape[1]
    nh = a_log.shape[1]

    def one(ss):
        xbc = ss[..., sw:sw + xbc_w]
        xbc = _centred_dwconv3(xbc, conv_w, conv_b)
        return xbc * jax.nn.sigmoid(xbc), ss[..., sw + xbc_w:]

    xbc_c, dt_c = one(ss_c)
    xbc_x, dt_x = one(ss_x)
    xbc = jnp.concatenate([xbc_c, xbc_x], axis=1)
    dt_raw = jnp.concatenate([dt_c, dt_x], axis=1)
    b, tt, _ = xbc.shape
    nbc = (xbc_w - sw) // 2
    xs, bm, cm = xbc[..., :sw], xbc[..., sw:sw + nbc], xbc[..., sw + nbc:]
    dt = jax.nn.softplus(dt_raw.reshape(b, tt, 2, nh) + dt_bias)
    la = dt * -jnp.exp(a_log)
    return xs, bm, cm, dt, la, ss_x[..., :sw]


def _col_form(zz, ng):
    b, tt, _, nh = zz.shape
    return jnp.transpose(zz.reshape(b, tt, 2, ng, nh // ng), (2, 0, 3, 1, 4))


def _row_form(zz, ng):
    b, tt, _, nh = zz.shape
    return jnp.transpose(zz.reshape(b, tt, 2, ng, nh // ng), (2, 0, 3, 4, 1))


def _ssd_finish(y, xs, z, d_skip, norm_w, ng):
    b, t, sw = y.shape
    y = y + jnp.repeat(d_skip, sw // d_skip.shape[0]) * xs
    y = y * (z * jax.nn.sigmoid(z))
    yg = y.reshape(b, t, ng, sw // ng)
    yg = yg * lax.rsqrt(jnp.mean(yg * yg, axis=-1, keepdims=True) + RMS_EPS)
    return yg.reshape(b, t, sw) * norm_w


def kernel(x, c, ctx, c_ctx, w_mod, b_mod, norm1_g, norm2_g, w_in, w_out, rwkv_mu, rwkv_w0, rwkv_w2, rwkv_a0, rwkv_a2, rwkv_g2, rwkv_k_k, rwkv_k_a, rwkv_r_k, rwkv_ln_w, rwkv_ln_b, ssm_conv_w, ssm_conv_b, ssm_dt_bias, ssm_a_log, ssm_d, ssm_norm_w, peer_wq, peer_k1, peer_k2, peer_u, peer_v, final_g):
    assert w_in.shape[0] == 1, "single-layer block"
    nb, t, d = x.shape
    cl = ctx.shape[1]
    mod_rows = 8
    s = jnp.concatenate([c, c_ctx[None], jnp.zeros((mod_rows - nb - 1, d), F32)], axis=0)
    mod3 = mod_matmul(s, w_mod[0], b_mod[0]).reshape(mod_rows * N_MOD, 1, d)

    tm = 512
    nctx_blk = nb * cl // tm
    x_blk = t // tm
    h_all = jnp.concatenate([ctx.reshape(nb * cl, d), x.reshape(nb * t, d)], axis=0)
    in_cols = w_in.shape[2]
    in_pad = -in_cols % 768
    w_in_p = jnp.pad(w_in[0], ((0, 0), (0, in_pad))).astype(BF16)
    proj = normmod_matmul(
        h_all, norm1_g[0], mod3, w_in_p,
        row_of_block=lambda i: jnp.where(i < nctx_blk, nb, (i - nctx_blk) // x_blk),
        shift_idx=0, scale_idx=1, n_mod=N_MOD, tm=tm, tn=768)
    p_c = proj[:nb * cl].reshape(nb, cl, -1)
    p_x = proj[nb * cl:].reshape(nb, t, -1)

    cw = rwkv_k_k.shape[1]
    rc = rwkv_mu.shape[1]
    rw_c, rw_x = p_c[..., :rc], p_x[..., :rc]
    cols = jnp.concatenate([rw_c, rw_x], axis=1)
    shifted = jnp.concatenate([_seq_token_shift(rw_c), _grid_token_shift(rw_x, t // GRID_W)], axis=1)
    r, v, kk, ld, kd, bb, bonus, gate = _rwkv_prepare(
        cols, shifted, cl, rwkv_mu[0], rwkv_w0[0], rwkv_w2[0], rwkv_a0[0], rwkv_a2[0], rwkv_g2[0],
        rwkv_k_k[0], rwkv_k_a[0], rwkv_r_k[0])
    o = rwkv_scan(r, v, kk, ld, kd, bb, ctx_len=cl)
    y_rwkv = _rwkv_finish(o[0] + o[1], bonus, gate, rwkv_ln_w[0], rwkv_ln_b[0])

    sw = ssm_norm_w.shape[1]
    ss_cols = in_cols - rc
    xs, bm, cm, dt, la, z = _ssd_prepare(
        p_c[..., rc:rc + ss_cols], p_x[..., rc:rc + ss_cols], sw,
        ssm_conv_w[0], ssm_conv_b[0], ssm_dt_bias[0], ssm_a_log[0])
    ng = bm.shape[-1] // SSM_STATE
    ys = ssd_scan(xs, bm, cm, _col_form(dt, ng), _col_form(la, ng), _row_form(dt, ng),
                  _row_form(la, ng), ctx_len=cl)
    y_ssd = _ssd_finish(ys[0] + ys[1], xs[:, cl:], z, ssm_d[0], ssm_norm_w[0], ng)

    y = jnp.concatenate([y_rwkv, y_ssd], axis=-1).reshape(nb * t, d).astype(BF16)
    by_batch = lambda i: i // x_blk
    h1 = matmul_gated_resid(y, w_out[0].astype(BF16), x.reshape(nb * t, d), mod3,
                            row_of_block=by_batch, gate_idx=2, n_mod=N_MOD, tm=tm, tn=1024)

    q, xn2 = normmod_matmul(h1, norm2_g[0], mod3, peer_wq[0].astype(BF16), row_of_block=by_batch,
                            shift_idx=3, scale_idx=4, n_mod=N_MOD, tm=tm, tn=512, emit_xn=True)
    s1, s2, e1, e2, tau = peer_scores(q, peer_k1[0], peer_k2[0], topk=PEER_TOPK)
    wt = peer_weights(peer_u[0].astype(BF16), xn2, s1, s2, e1, e2, tau)
    p = peer_out(wt, peer_v[0].astype(BF16))
    tf = 256
    out = final_norm(h1, p, mod3, final_g, row_of_block=lambda i: i // (t // tf), gate_idx=5,
                     n_mod=N_MOD, tm=tf)
    return out.reshape(nb, t, d)
```

```python
import functools
import math

import jax
import jax.numpy as jnp
from jax import lax
from jax.experimental import pallas as pl
from jax.experimental.pallas import tpu as pltpu

F32 = jnp.float32
BF16 = jnp.bfloat16
HI = lax.Precision.HIGHEST

RWKV_CHUNK = 64
RWKV_HEAD = 64
PAIR = 2 * RWKV_HEAD


def _mm(a, b, precision=None):
    return lax.dot_general(a, b, (((1,), (0,)), ((), ())), precision=precision,
                           preferred_element_type=F32)


def _mm_nt(a, b, precision=None):
    return lax.dot_general(a, b, (((1,), (1,)), ((), ())), precision=precision,
                           preferred_element_type=F32)


def _rwkv_masks(is_bwd):
    rho = lax.broadcasted_iota(jnp.int32, (PAIR, PAIR), 0)
    sig = lax.broadcasted_iota(jnp.int32, (PAIR, PAIR), 1)
    same64 = (rho // 64) == (sig // 64)
    same32 = (rho // 32) == (sig // 32)
    same16 = (rho // 16) == (sig // 16)
    sgn = jnp.where(is_bwd, -1, 1).astype(jnp.int32)
    before = ((rho - sig) * sgn > 0) & same64
    eye = rho == sig
    t = lax.broadcasted_iota(jnp.int32, (RWKV_CHUNK, RWKV_CHUNK), 0)
    s = lax.broadcasted_iota(jnp.int32, (RWKV_CHUNK, RWKV_CHUNK), 1)
    tri = jnp.where((t - s) * sgn >= 0, 1.0, 0.0).astype(F32)
    return dict(same64=same64, d16=same16, off32=same32 & ~same16, off64=same64 & ~same32,
                before=before, before_eq=before | eye, eye=eye, tri=tri)


def _each(f, *lists):
    return [f(*xs) for xs in zip(*lists)]


def _rwkv_chunk(r, v, kk, ld, kd, bb, S, m):
    zero = jnp.zeros((), F32)
    bf = lambda z: z.astype(BF16)
    bd, before, eye = m["same64"], m["before"], m["eye"]
    before_eq2 = jnp.concatenate([m["before_eq"]] * 2, axis=1)
    stack2 = lambda x: jnp.where(bd, jnp.concatenate([x, x], axis=0), zero)

    cum = _each(lambda x: _mm(m["tri"], x, HI), ld)
    tot = _each(lambda x: jnp.sum(x, axis=0, keepdims=True), ld)
    e_inv = _each(lambda c: jnp.exp(-c), cum)
    e_end = _each(lambda c, t: jnp.exp(t - c), cum, tot)
    a2 = _each(lambda k, c, x: stack2(-k * jnp.exp(c - x)), kk, cum, ld)
    r2 = _each(lambda x, c: stack2(x * jnp.exp(c)), r, cum)
    b2 = _each(lambda x, e: stack2(x * e), bb, e_inv)
    k2 = _each(lambda x, e: stack2(x * e), kd, e_inv)
    bkg = _each(lambda x, y, e: bf(jnp.concatenate([stack2(x * e), stack2(y * e)], axis=0)),
                bb, kd, e_end)
    v2 = _each(stack2, v)
    sc = _each(lambda a, rr, b, k: _mm_nt(bf(jnp.concatenate([a, rr], axis=0)),
                                          bf(jnp.concatenate([b, k], axis=0))), a2, r2, b2, k2)
    mm = _each(lambda s: jnp.where(before, s[:PAIR, :PAIR], zero), sc)
    nn = _each(lambda s: bf(jnp.where(before, s[:PAIR, PAIR:], zero)), sc)
    pq = _each(lambda s: bf(jnp.where(before_eq2, s[PAIR:], zero)), sc)

    md = _each(lambda x: bf(jnp.where(m["d16"], x, zero)), mm)
    m2 = _each(lambda x: bf(_mm(x, x)), md)
    t_inv = _each(lambda x: jnp.where(eye, 1.0, zero) + x, md)
    t_inv = _each(lambda t, p: t + _mm(bf(t), p), t_inv, m2)
    m4 = _each(lambda x: bf(_mm(x, x)), m2)
    t_inv = _each(lambda t, p: t + _mm(bf(t), p), t_inv, m4)
    m8 = _each(lambda x: bf(_mm(x, x)), m4)
    t_inv = _each(lambda t, p: t + _mm(bf(t), p), t_inv, m8)
    for off in ("off32", "off64"):
        mo = _each(lambda x: bf(jnp.where(m[off], x, zero)), mm)
        tb = _each(bf, t_inv)
        mt = _each(lambda a, b: bf(_mm(a, b)), mo, tb)
        t_inv = _each(lambda t, a, b: t + _mm(a, b), t_inv, tb, mt)
    tb = _each(bf, t_inv)

    at = _each(lambda x: x.T, a2)
    vt = _each(lambda x: bf(x.T), v2)
    rt = _each(lambda x: x.T, r2)
    nvt = _each(_mm_nt, vt, nn)
    x1 = _each(lambda a, n, t: _mm_nt(bf(jnp.concatenate([a, n], axis=0)), t), at, nvt, tb)
    atp = _each(lambda x: bf(x[:PAIR]), x1)
    cv = _each(lambda x, y: jnp.concatenate([bf(x[PAIR:]), y], axis=1), x1, vt)
    g = _each(lambda a, b, t: _mm(a, b[:PAIR]) + jnp.where(eye, jnp.exp(t), zero), atp, bkg, tot)
    rtp = _each(lambda x, a, p: x + _mm_nt(a, p[:, :PAIR]), rt, atp, pq)
    c2 = _each(_mm, cv, bkg)
    c3 = _each(_mm_nt, cv, pq)
    so = _each(lambda s, a, b: _mm(bf(s), bf(jnp.concatenate([a, b], axis=1))), S, g, rtp)
    s_new = _each(lambda x, c: x[:, :PAIR] + c, so, c2)
    o2 = _each(lambda x, c: (x[:, PAIR:] + c).T, so, c3)
    return _each(lambda x: x[:RWKV_CHUNK] + x[RWKV_CHUNK:], o2), s_new


def _rwkv_kernel(r_ref, v_ref, kk_ref, ld_ref, kd_ref, bb_ref, o_ref, s_ref, *, npair, nch):
    is_bwd = pl.program_id(0) == 1
    blk = pl.program_id(3)

    @pl.when(blk == 0)
    def _():
        s_ref[...] = jnp.zeros_like(s_ref)

    m = _rwkv_masks(is_bwd)
    lanes = [slice(p * PAIR, (p + 1) * PAIR) for p in range(npair)]

    def chunk(c, carry):
        ci = jnp.where(is_bwd, nch - 1 - c, c)
        rows = pl.ds(pl.multiple_of(ci * RWKV_CHUNK, RWKV_CHUNK), RWKV_CHUNK)
        o, s_new = _rwkv_chunk(
            [r_ref[0, rows, ln] for ln in lanes], [v_ref[0, rows, ln] for ln in lanes],
            [kk_ref[0, rows, ln] for ln in lanes], [ld_ref[0, 0, rows, ln] for ln in lanes],
            [kd_ref[0, 0, rows, ln] for ln in lanes], [bb_ref[0, 0, rows, ln] for ln in lanes],
            [s_ref[p] for p in range(npair)], m)
        for p in range(npair):
            s_ref[p] = s_new[p]
            o_ref[0, 0, rows, lanes[p]] = o[p]
        return carry

    lax.fori_loop(0, nch, chunk, 0)


def rwkv_scan(r, v, kk, ld, kd, bb, *, ctx_len, tb=256, lanes=1024, interpret=False):
    b, tt, c = r.shape
    assert ctx_len % tb == 0 and tt % tb == 0 and c % lanes == 0 and lanes % PAIR == 0
    nctx = ctx_len // tb
    nblk = tt // tb

    def tmap(d, k):
        fwd = k
        bwd = jnp.where(k < nctx, nctx - 1 - k, nblk - 1 - (k - nctx))
        return jnp.where(d == 1, bwd, fwd)

    shared = pl.BlockSpec((1, tb, lanes), lambda d, bi, h, k: (bi, tmap(d, k), h))
    direc = pl.BlockSpec((1, 1, tb, lanes), lambda d, bi, h, k: (d, bi, tmap(d, k), h))
    out = pl.BlockSpec((1, 1, tb, lanes),
                       lambda d, bi, h, k: (d, bi, tmap(d, jnp.maximum(k, nctx)) - nctx, h))
    npair = lanes // PAIR
    return pl.pallas_call(
        functools.partial(_rwkv_kernel, npair=npair, nch=tb // RWKV_CHUNK),
        grid=(2, b, c // lanes, nblk),
        in_specs=[shared, shared, shared, direc, direc, direc],
        out_specs=out,
        out_shape=jax.ShapeDtypeStruct((2, b, tt - ctx_len, c), F32),
        scratch_shapes=[pltpu.VMEM((npair, PAIR, PAIR), F32)],
        compiler_params=pltpu.CompilerParams(
            dimension_semantics=("arbitrary", "arbitrary", "arbitrary", "arbitrary")),
        name="rwkv_scan",
        interpret=interpret,
    )(r, v, kk, ld, kd, bb)


def _time_block_map(nctx, nblk):
    def tmap(d, k):
        bwd = jnp.where(k < nctx, nctx - 1 - k, nblk - 1 - (k - nctx))
        return jnp.where(d == 1, bwd, k)
    return tmap


SSD_P = 64


def _ssd_kernel(xs_ref, bm_ref, cm_ref, dtc_ref, lac_ref, dtr_ref, lar_ref, y_ref, st_ref, *, chunk):
    is_bwd = pl.program_id(0) == 1
    blk = pl.program_id(3)

    @pl.when(blk == 0)
    def _():
        st_ref[...] = jnp.zeros_like(st_ref)

    sgn = jnp.where(is_bwd, -1, 1).astype(jnp.int32)
    l_i = lax.broadcasted_iota(jnp.int32, (chunk, chunk), 0)
    s_i = lax.broadcasted_iota(jnp.int32, (chunk, chunk), 1)
    incl = (l_i - s_i) * sgn >= 0
    tri = jnp.where(incl, 1.0, 0.0).astype(F32)
    lane = lax.broadcasted_iota(jnp.int32, (chunk, PAIR), 1)
    first = lane < SSD_P
    first_row = first[:1]

    bmat = bm_ref[0]
    cmat = cm_ref[0]
    dtc = dtc_ref[0, 0, 0]
    lac = lac_ref[0, 0, 0]
    dtr = dtr_ref[0, 0, 0]
    lar = lar_ref[0, 0, 0]
    nh = dtc.shape[-1]

    cmat_b = cmat.astype(BF16)
    cb = _mm_nt(cmat_b, bmat.astype(BF16))
    b_t = bmat.T.astype(BF16)
    cum_col = _mm(tri, lac, HI)
    cum_row = _mm_nt(lar, tri, HI)
    tot = jnp.sum(lac, axis=0, keepdims=True)
    coef = jnp.exp(tot - cum_col) * dtc
    ecum = jnp.exp(cum_col)
    etot = jnp.exp(tot)

    for q in range(nh // 2):
        lanes = slice(q * PAIR, (q + 1) * PAIR)
        h0, h1 = 2 * q, 2 * q + 1
        xp = xs_ref[0, :, lanes]
        xp_b = xp.astype(BF16)
        st = st_ref[:, lanes]
        ys = []
        for h in (h0, h1):
            seg = cum_col[:, h:h + 1] - cum_row[h:h + 1, :]
            dec = jnp.exp(jnp.where(incl, seg, -jnp.inf))
            ys.append(_mm((cb * dec * dtr[h:h + 1, :]).astype(BF16), xp_b))
        y = jnp.where(first, ys[0], ys[1])
        y = y + (_mm(cmat_b, st.astype(BF16))
                 * jnp.where(first, ecum[:, h0:h0 + 1], ecum[:, h1:h1 + 1]))
        xc = xp * jnp.where(first, coef[:, h0:h0 + 1], coef[:, h1:h1 + 1])
        st_ref[:, lanes] = (st * jnp.where(first_row, etot[:, h0:h0 + 1], etot[:, h1:h1 + 1])
                            + _mm(b_t, xc.astype(BF16)))
        y_ref[0, 0, :, lanes] = y


def ssd_scan(xbc, dtc, lac, dtr, lar, *, ctx_len, nst, chunk=256, interpret=False):
    b, tt, _ = xbc.shape
    ng, nh = dtc.shape[2], dtc.shape[4]
    gw = nh * SSD_P
    width = ng * gw
    assert xbc.shape[2] == width + 2 * ng * nst and width % nst == 0
    assert ctx_len % chunk == 0 and tt % chunk == 0 and nh % 2 == 0
    nctx, nblk = ctx_len // chunk, tt // chunk
    tmap = _time_block_map(nctx, nblk)
    b0 = width // nst
    c0 = b0 + ng
    return pl.pallas_call(
        functools.partial(_ssd_kernel, chunk=chunk),
        grid=(2, b, ng, nblk),
        in_specs=[
            pl.BlockSpec((1, chunk, gw), lambda d, bi, g, k: (bi, tmap(d, k), g)),
            pl.BlockSpec((1, chunk, nst), lambda d, bi, g, k: (bi, tmap(d, k), b0 + g)),
            pl.BlockSpec((1, chunk, nst), lambda d, bi, g, k: (bi, tmap(d, k), c0 + g)),
            pl.BlockSpec((1, 1, 1, chunk, nh), lambda d, bi, g, k: (d, bi, g, tmap(d, k), 0)),
            pl.BlockSpec((1, 1, 1, chunk, nh), lambda d, bi, g, k: (d, bi, g, tmap(d, k), 0)),
            pl.BlockSpec((1, 1, 1, nh, chunk), lambda d, bi, g, k: (d, bi, g, 0, tmap(d, k))),
            pl.BlockSpec((1, 1, 1, nh, chunk), lambda d, bi, g, k: (d, bi, g, 0, tmap(d, k))),
        ],
        out_specs=pl.BlockSpec(
            (1, 1, chunk, gw),
            lambda d, bi, g, k: (d, bi, tmap(d, jnp.maximum(k, nctx)) - nctx, g)),
        out_shape=jax.ShapeDtypeStruct((2, b, tt - ctx_len, width), F32),
        scratch_shapes=[pltpu.VMEM((nst, gw), F32)],
        compiler_params=pltpu.CompilerParams(
            dimension_semantics=("arbitrary", "arbitrary", "arbitrary", "arbitrary")),
        name="ssd_scan",
        interpret=interpret,
    )(xbc, xbc, xbc, dtc, lac, dtr, lar)


RMS_EPS = 1e-6
VMEM_LIMIT = 56 * 1024 * 1024


def _mod_kernel(s_ref, w_ref, b_ref, o_ref):
    s = s_ref[...]
    s = s * jax.nn.sigmoid(s)
    o_ref[...] = _mm(s, w_ref[...], HI) + b_ref[...]


def mod_matmul(s, w, bias, *, tn=512, interpret=False):
    m, k = s.shape
    n = w.shape[1]
    return pl.pallas_call(
        _mod_kernel,
        grid=(n // tn,),
        in_specs=[pl.BlockSpec((m, k), lambda j: (0, 0)),
                  pl.BlockSpec((k, tn), lambda j: (0, j)),
                  pl.BlockSpec((1, tn), lambda j: (0, j))],
        out_specs=pl.BlockSpec((m, tn), lambda j: (0, j)),
        out_shape=jax.ShapeDtypeStruct((m, n), F32),
        compiler_params=pltpu.CompilerParams(dimension_semantics=("arbitrary",),
                                             vmem_limit_bytes=VMEM_LIMIT),
        name="mod_matmul",
        interpret=interpret,
    )(s, w, bias.reshape(1, n))


def _normmod_matmul_kernel(h_ref, g_ref, sh_ref, sc_ref, w_ref, o_ref, *rest, emit_xn):
    xn_ref = rest[-1]

    @pl.when(pl.program_id(1) == 0)
    def _():
        h = h_ref[...]
        hn = h * lax.rsqrt(jnp.mean(h * h, axis=-1, keepdims=True) + RMS_EPS)
        xn = (hn * g_ref[...]) * (1.0 + sc_ref[0]) + sh_ref[0]
        xn_ref[...] = xn.astype(BF16)
        if emit_xn:
            rest[0][...] = xn.astype(BF16)

    o_ref[...] = _mm(xn_ref[...], w_ref[...])


def normmod_matmul(h, g, mod3, w, *, row_of_block, shift_idx, scale_idx, n_mod, tm, tn,
                   emit_xn=False, interpret=False):
    rows, d = h.shape
    n = w.shape[1]
    assert rows % tm == 0 and n % tn == 0
    out_shape = [jax.ShapeDtypeStruct((rows, n), F32)]
    out_specs = [pl.BlockSpec((tm, tn), lambda i, j: (i, j))]
    if emit_xn:
        out_shape.append(jax.ShapeDtypeStruct((rows, d), BF16))
        out_specs.append(pl.BlockSpec((tm, d), lambda i, j: (i, 0)))
    res = pl.pallas_call(
        functools.partial(_normmod_matmul_kernel, emit_xn=emit_xn),
        grid=(rows // tm, n // tn),
        in_specs=[
            pl.BlockSpec((tm, d), lambda i, j: (i, 0)),
            pl.BlockSpec((1, d), lambda i, j: (0, 0)),
            pl.BlockSpec((1, 1, d), lambda i, j: (row_of_block(i) * n_mod + shift_idx, 0, 0)),
            pl.BlockSpec((1, 1, d), lambda i, j: (row_of_block(i) * n_mod + scale_idx, 0, 0)),
            pl.BlockSpec((d, tn), lambda i, j: (0, j)),
        ],
        out_specs=out_specs,
        out_shape=out_shape,
        scratch_shapes=[pltpu.VMEM((tm, d), BF16)],
        compiler_params=pltpu.CompilerParams(dimension_semantics=("arbitrary", "arbitrary"),
                                             vmem_limit_bytes=VMEM_LIMIT),
        name="normmod_matmul",
        interpret=interpret,
    )(h, g.reshape(1, d), mod3, mod3, w)
    return res if emit_xn else res[0]


def _matmul_gated_resid_kernel(ya_ref, yb_ref, w_ref, x_ref, gt_ref, o_ref):
    ka = ya_ref.shape[1]
    acc = _mm(ya_ref[...], w_ref[:ka, :]) + _mm(yb_ref[...], w_ref[ka:, :])
    o_ref[...] = x_ref[...] + gt_ref[0] * acc


def matmul_gated_resid(ya, yb, w, resid, mod3, *, row_of_block, gate_idx, n_mod, tm, tn,
                       interpret=False):
    rows, ka = ya.shape
    kb = yb.shape[1]
    k, n = w.shape
    assert rows % tm == 0 and n % tn == 0 and ka + kb == k
    return pl.pallas_call(
        _matmul_gated_resid_kernel,
        grid=(rows // tm, n // tn),
        in_specs=[
            pl.BlockSpec((tm, ka), lambda i, j: (i, 0)),
            pl.BlockSpec((tm, kb), lambda i, j: (i, 0)),
            pl.BlockSpec((k, tn), lambda i, j: (0, j)),
            pl.BlockSpec((tm, tn), lambda i, j: (i, j)),
            pl.BlockSpec((1, 1, tn), lambda i, j: (row_of_block(i) * n_mod + gate_idx, 0, j)),
        ],
        out_specs=pl.BlockSpec((tm, tn), lambda i, j: (i, j)),
        out_shape=jax.ShapeDtypeStruct((rows, n), F32),
        compiler_params=pltpu.CompilerParams(dimension_semantics=("arbitrary", "arbitrary"),
                                             vmem_limit_bytes=VMEM_LIMIT),
        name="matmul_gated_resid",
        interpret=interpret,
    )(ya, yb, w, resid, mod3)


def _top_values(s, k):
    n = s.shape[0]
    row = lax.broadcasted_iota(jnp.int32, s.shape, 0)
    vals = []
    for _ in range(k):
        mx = jnp.max(s, axis=0, keepdims=True)
        first = jnp.min(jnp.where(s == mx, row, n), axis=0, keepdims=True)
        s = jnp.where(row == first, -jnp.inf, s)
        vals.append(mx)
    return jnp.concatenate(vals, axis=0)


def _peer_score_kernel(q_ref, k1_ref, k2_ref, s1_ref, s2_ref, e1_ref, e2_ref, tau_ref, *, topk):
    nheads, nkeys, qd = k1_ref.shape
    taus = []
    for h in range(nheads):
        q1 = q_ref[:, h * 2 * qd:(h * 2 + 1) * qd]
        q2 = q_ref[:, (h * 2 + 1) * qd:(h * 2 + 2) * qd]
        s1 = _mm_nt(k1_ref[h], q1, HI)
        s2 = _mm_nt(k2_ref[h], q2, HI)
        v1 = _top_values(s1, topk)
        v2 = _top_values(s2, topk)
        cand = jnp.concatenate([v1[a:a + 1] + v2[:topk // (a + 1)] for a in range(topk)], axis=0)
        best = _top_values(cand, topk)
        z = jnp.sum(jnp.exp(best - best[:1]), axis=0, keepdims=True)
        s1_ref[h] = s1
        s2_ref[h] = s2
        e1_ref[h] = jnp.exp(s1 - v1[:1])
        e2_ref[h] = jnp.exp(s2 - v2[:1]) / z
        taus.append(best[topk - 1:topk])
    tau_ref[...] = jnp.concatenate(taus, axis=0)


def peer_scores(q, k1, k2, *, topk, tb=256, interpret=False):
    t = q.shape[0]
    nheads, nkeys, qd = k1.shape
    big = pl.BlockSpec((nheads, nkeys, tb), lambda i: (0, 0, i))
    big_shape = jax.ShapeDtypeStruct((nheads, nkeys, t), F32)
    return pl.pallas_call(
        functools.partial(_peer_score_kernel, topk=topk),
        grid=(t // tb,),
        in_specs=[pl.BlockSpec((tb, q.shape[1]), lambda i: (i, 0)),
                  pl.BlockSpec(k1.shape, lambda i: (0, 0, 0)),
                  pl.BlockSpec(k2.shape, lambda i: (0, 0, 0))],
        out_specs=[big, big, big, big, pl.BlockSpec((nheads, tb), lambda i: (0, i))],
        out_shape=[big_shape] * 4 + [jax.ShapeDtypeStruct((nheads, t), F32)],
        compiler_params=pltpu.CompilerParams(dimension_semantics=("arbitrary",),
                                             vmem_limit_bytes=VMEM_LIMIT),
        name="peer_scores",
        interpret=interpret,
    )(q, k1, k2)


def _peer_weight_kernel(u_ref, x_ref, s1_ref, s2_ref, e1_ref, e2_ref, tau_ref, w_ref):
    nheads, nkeys, _ = s2_ref.shape
    a = _mm_nt(u_ref[...], x_ref[...])
    act = 0.5 * a * (1.0 + lax.erf(a * (1.0 / math.sqrt(2.0))))
    for ii in range(u_ref.shape[0] // nkeys):
        gate = jnp.zeros((nkeys, x_ref.shape[0]), F32)
        for h in range(nheads):
            pair = s1_ref[h, ii:ii + 1, :] + s2_ref[h]
            gate = gate + jnp.where(pair >= tau_ref[h:h + 1, :],
                                    e1_ref[h, ii:ii + 1, :] * e2_ref[h], 0.0)
        rows = slice(ii * nkeys, (ii + 1) * nkeys)
        w_ref[rows, :] = (act[rows] * gate).astype(w_ref.dtype)


def peer_weights(u, xn, s1, s2, e1, e2, tau, *, te=1024, tb=512, interpret=False):
    ne, d = u.shape
    t = xn.shape[0]
    nheads, nkeys, _ = s1.shape
    ni = te // nkeys
    s1_spec = pl.BlockSpec((nheads, ni, tb), lambda i, j: (0, j, i))
    s2_spec = pl.BlockSpec((nheads, nkeys, tb), lambda i, j: (0, 0, i))
    return pl.pallas_call(
        _peer_weight_kernel,
        grid=(t // tb, ne // te),
        in_specs=[pl.BlockSpec((te, d), lambda i, j: (j, 0)),
                  pl.BlockSpec((tb, d), lambda i, j: (i, 0)),
                  s1_spec, s2_spec, s1_spec, s2_spec,
                  pl.BlockSpec((nheads, tb), lambda i, j: (0, i))],
        out_specs=pl.BlockSpec((te, tb), lambda i, j: (j, i)),
        out_shape=jax.ShapeDtypeStruct((ne, t), BF16),
        compiler_params=pltpu.CompilerParams(dimension_semantics=("arbitrary", "arbitrary"),
                                             vmem_limit_bytes=VMEM_LIMIT),
        name="peer_weights",
        interpret=interpret,
    )(u, xn, s1, s2, e1, e2, tau)


def _peer_out_kernel(w_ref, v_ref, o_ref):
    @pl.when(pl.program_id(1) == 0)
    def _():
        o_ref[...] = jnp.zeros_like(o_ref)

    o_ref[...] += lax.dot_general(w_ref[...], v_ref[...], (((0,), (0,)), ((), ())),
                                  preferred_element_type=F32)


def peer_out(wt, v, *, tb=512, te=1024, interpret=False):
    ne, t = wt.shape
    d = v.shape[1]
    return pl.pallas_call(
        _peer_out_kernel,
        grid=(t // tb, ne // te),
        in_specs=[pl.BlockSpec((te, tb), lambda i, j: (j, i)),
                  pl.BlockSpec((te, d), lambda i, j: (j, 0))],
        out_specs=pl.BlockSpec((tb, d), lambda i, j: (i, 0)),
        out_shape=jax.ShapeDtypeStruct((t, d), F32),
        compiler_params=pltpu.CompilerParams(dimension_semantics=("arbitrary", "arbitrary"),
                                             vmem_limit_bytes=VMEM_LIMIT),
        name="peer_out",
        interpret=interpret,
    )(wt, v)


def _final_kernel(h_ref, p_ref, gt_ref, g_ref, o_ref):
    h = h_ref[...] + gt_ref[0] * p_ref[...]
    o_ref[...] = h * lax.rsqrt(jnp.mean(h * h, axis=-1, keepdims=True) + RMS_EPS) * g_ref[...]


def final_norm(h, p, mod3, g, *, row_of_block, gate_idx, n_mod, tm=256, interpret=False):
    rows, d = h.shape
    blk = pl.BlockSpec((tm, d), lambda i: (i, 0))
    return pl.pallas_call(
        _final_kernel,
        grid=(rows // tm,),
        in_specs=[blk, blk,
                  pl.BlockSpec((1, 1, d), lambda i: (row_of_block(i) * n_mod + gate_idx, 0, 0)),
                  pl.BlockSpec((1, d), lambda i: (0, 0))],
        out_specs=blk,
        out_shape=jax.ShapeDtypeStruct((rows, d), F32),
        compiler_params=pltpu.CompilerParams(dimension_semantics=("arbitrary",),
                                             vmem_limit_bytes=VMEM_LIMIT),
        name="final_norm",
        interpret=interpret,
    )(h, p, mod3, g.reshape(1, d))


N_MOD = 6
GRID_W = 64
RWKV_GN_EPS = 64e-5
SSM_STATE = 128
PEER_TOPK = 16


LANES = 128
TOK_BLK = 256


def _block_ones(group):
    i = lax.broadcasted_iota(jnp.int32, (LANES, LANES), 0) // group
    j = lax.broadcasted_iota(jnp.int32, (LANES, LANES), 1) // group
    return jnp.where(i == j, 1.0, 0.0).astype(F32)


def _segsum(x, ones):
    return jnp.concatenate([_mm(x[:, i * LANES:(i + 1) * LANES], ones, HI)
                            for i in range(x.shape[1] // LANES)], axis=1)


def _softplus(x):
    return jnp.maximum(x, 0.0) + jnp.log(1.0 + jnp.exp(-jnp.abs(x)))


def _shift_mix(cur_ref, prev_ref, next_ref, mu_ref, is_ctx, top, bottom, grid_w):
    cur = cur_ref[...]
    n, w = cur.shape
    row = lax.broadcasted_iota(jnp.int32, (n, w), 0)
    lane = lax.broadcasted_iota(jnp.int32, (n, w), 1)
    before = pltpu.roll(cur, 1, axis=0)
    after = pltpu.roll(cur, n - 1, axis=0)
    col = row % grid_w
    left = jnp.where(col == 0, 0.0, before)
    right = jnp.where(col == grid_w - 1, 0.0, after)
    up = jnp.concatenate([prev_ref[...] * (1.0 - top), cur[:n - grid_w]], axis=0)
    down = jnp.concatenate([cur[grid_w:], next_ref[...] * (1.0 - bottom)], axis=0)
    slot = lane % 4
    sh_grid = jnp.where(slot == 0, left, jnp.where(slot == 1, right, jnp.where(slot == 2, up, down)))
    sh_seq = jnp.where(lane % 2 == 0, jnp.where(row == 0, 0.0, before),
                       jnp.where(row == n - 1, 0.0, after))
    shifted = jnp.where(is_ctx, sh_seq, sh_grid)
    return cur + mu_ref[...] * (shifted - cur)


def _rwkv_prep_kernel(r_c, r_p, r_n, k_c, k_p, k_n, v_c, v_p, v_n, l_c, l_p, l_n,
                      mu_r, mu_k, mu_v, mu_l, w2_ref, a2_ref, g2_ref, w0_ref, a0_ref,
                      kk_ref, ka_ref, rk_ref,
                      r_o, v_o, kkn_o, ld_o, kd_o, bb_o, bonus_o, gate_o,
                      *, nctx_blk, blk_per_seq, grid_w):
    i = pl.program_id(0)
    is_ctx = i < nctx_blk
    tblk = (i - nctx_blk) % blk_per_seq
    top = jnp.where(tblk == 0, 1.0, 0.0)
    bottom = jnp.where(tblk == blk_per_seq - 1, 1.0, 0.0)
    mix = functools.partial(_shift_mix, is_ctx=is_ctx, top=top, bottom=bottom, grid_w=grid_w)
    r = mix(r_c, r_p, r_n, mu_r)
    k = mix(k_c, k_p, k_n, mu_k)
    v = mix(v_c, v_p, v_n, mu_v)
    low = mix(l_c, l_p, l_n, mu_l)
    cw = r.shape[1]

    w_low = _mm(jnp.tanh(low).astype(BF16), w2_ref[...])
    a_low = _mm(low.astype(BF16), a2_ref[...])
    gate_o[0] = _mm(jax.nn.sigmoid(low).astype(BF16), g2_ref[...])

    ones = _block_ones(RWKV_HEAD)
    kraw = k * kk_ref[...]
    kkn = kraw * lax.rsqrt(jnp.maximum(_segsum(kraw * kraw, ones), 1e-24))
    ksum = jnp.zeros_like(k)
    for d in range(2):
        cols = slice(d * cw, (d + 1) * cw)
        w = -_softplus(-(w0_ref[:, cols] + w_low[:, cols])) - 0.5
        a = jax.nn.sigmoid(a0_ref[:, cols] + a_low[:, cols])
        k_dir = k * (1.0 + (a - 1.0) * ka_ref[...])
        ld_o[d, 0] = -jnp.exp(w)
        kd_o[d, 0] = k_dir
        bb_o[d, 0] = kkn * a
        ksum = ksum + k_dir
    r_o[0] = r
    v_o[0] = v
    kkn_o[0] = kkn
    bonus_o[0] = _segsum(r * ksum * rk_ref[...], ones) * v


def rwkv_prep(proj, mu_p, w2g, a2g, g2g, w0g, a0g, k_k, k_a, r_k, *, nb, cl, t, cw, lr_off, lr_w,
              grid_w, cg=512, interpret=False):
    tb = TOK_BLK
    assert cl == tb and t % tb == 0 and tb % grid_w == 0 and cw % cg == 0 and lr_off % lr_w == 0
    nctx_blk = nb
    blk_per_seq = t // tb
    nblk = nctx_blk + nb * blk_per_seq
    hb = tb // grid_w
    last_h = nblk * hb - 1
    ncg = cw // cg

    def seg(col0, width):
        cb = col0 // width
        return [pl.BlockSpec((tb, width), lambda i, g, cb=cb, s=(width == cg): (i, cb + g * s)),
                pl.BlockSpec((grid_w, width),
                             lambda i, g, cb=cb, s=(width == cg): (jnp.maximum(i * hb - 1, 0), cb + g * s)),
                pl.BlockSpec((grid_w, width),
                             lambda i, g, cb=cb, s=(width == cg): (jnp.minimum((i + 1) * hb, last_h), cb + g * s))]

    def mu_spec(col0, width):
        cb = col0 // width
        return pl.BlockSpec((1, width), lambda i, g, cb=cb, s=(width == cg): (0, cb + g * s))

    def tok(i):
        b = jnp.where(i < nctx_blk, i, (i - nctx_blk) // blk_per_seq)
        tk = jnp.where(i < nctx_blk, 0, 1 + (i - nctx_blk) % blk_per_seq)
        return b, tk

    def out1(i, g):
        b, tk = tok(i)
        return (b, tk, g)

    def out2(i, g):
        b, tk = tok(i)
        return (0, b, tk, g)

    per_c = pl.BlockSpec((1, cg), lambda i, g: (0, g))
    one_shape = jax.ShapeDtypeStruct((nb, cl + t, cw), F32)
    two_shape = jax.ShapeDtypeStruct((2, nb, cl + t, cw), F32)
    one_spec = pl.BlockSpec((1, tb, cg), out1)
    two_spec = pl.BlockSpec((2, 1, tb, cg), out2)
    return pl.pallas_call(
        functools.partial(_rwkv_prep_kernel, nctx_blk=nctx_blk, blk_per_seq=blk_per_seq,
                          grid_w=grid_w),
        grid=(nblk, ncg),
        in_specs=(seg(0, cg) + seg(cw, cg) + seg(2 * cw, cg) + seg(lr_off, lr_w)
                  + [mu_spec(0, cg), mu_spec(cw, cg), mu_spec(2 * cw, cg), mu_spec(lr_off, lr_w)]
                  + [pl.BlockSpec((lr_w, 2 * cg), lambda i, g: (0, g)),
                     pl.BlockSpec((lr_w, 2 * cg), lambda i, g: (0, g)),
                     pl.BlockSpec((lr_w, cg), lambda i, g: (0, g)),
                     pl.BlockSpec((1, 2 * cg), lambda i, g: (0, g)),
                     pl.BlockSpec((1, 2 * cg), lambda i, g: (0, g)),
                     per_c, per_c, per_c]),
        out_specs=[one_spec, one_spec, one_spec, two_spec, two_spec, two_spec, one_spec, one_spec],
        out_shape=[one_shape] * 3 + [two_shape] * 3 + [one_shape] * 2,
        compiler_params=pltpu.CompilerParams(dimension_semantics=("arbitrary", "arbitrary"),
                                             vmem_limit_bytes=VMEM_LIMIT),
        name="rwkv_prep",
        interpret=interpret,
    )(*([proj] * 12), mu_p, mu_p, mu_p, mu_p, w2g, a2g, g2g, w0g, a0g,
      k_k.reshape(1, cw), k_a.reshape(1, cw), r_k.reshape(1, cw))


def _ssd_prep_kernel(c_ref, p_ref, n_ref, w_ref, b_ref, o_ref, *, nctx_blk, blk_per_seq):
    i = pl.program_id(0)
    is_ctx = i < nctx_blk
    tblk = (i - nctx_blk) % blk_per_seq
    first = jnp.logical_or(is_ctx, tblk == 0)
    last = jnp.logical_or(is_ctx, tblk == blk_per_seq - 1)
    cur = c_ref[...]
    n = cur.shape[0]
    row = lax.broadcasted_iota(jnp.int32, cur.shape, 0)
    halo = p_ref.shape[0]
    prev_row = p_ref[halo - 1:halo, :] * jnp.where(first, 0.0, 1.0)
    next_row = n_ref[0:1, :] * jnp.where(last, 0.0, 1.0)
    before = jnp.where(row == 0, prev_row, pltpu.roll(cur, 1, axis=0))
    after = jnp.where(row == n - 1, next_row, pltpu.roll(cur, n - 1, axis=0))
    y = w_ref[0:1, :] * before + w_ref[1:2, :] * cur + w_ref[2:3, :] * after + b_ref[...]
    o_ref[0] = y * jax.nn.sigmoid(y)


def ssd_prep(proj, conv_w, conv_b, *, nb, cl, t, col0, cg=512, interpret=False):
    tb = TOK_BLK
    halo = 8
    xbc_w = conv_w.shape[1]
    assert conv_w.shape[0] == 3 and cl == tb and t % tb == 0 and col0 % cg == 0 and xbc_w % cg == 0
    nctx_blk = nb
    blk_per_seq = t // tb
    nblk = nctx_blk + nb * blk_per_seq
    hb = tb // halo
    last_h = nblk * hb - 1
    cb = col0 // cg

    def out_map(i, g):
        b = jnp.where(i < nctx_blk, i, (i - nctx_blk) // blk_per_seq)
        tk = jnp.where(i < nctx_blk, 0, 1 + (i - nctx_blk) % blk_per_seq)
        return (b, tk, g)

    return pl.pallas_call(
        functools.partial(_ssd_prep_kernel, nctx_blk=nctx_blk, blk_per_seq=blk_per_seq),
        grid=(nblk, xbc_w // cg),
        in_specs=[pl.BlockSpec((tb, cg), lambda i, g: (i, cb + g)),
                  pl.BlockSpec((halo, cg), lambda i, g: (jnp.maximum(i * hb - 1, 0), cb + g)),
                  pl.BlockSpec((halo, cg), lambda i, g: (jnp.minimum((i + 1) * hb, last_h), cb + g)),
                  pl.BlockSpec((3, cg), lambda i, g: (0, g)),
                  pl.BlockSpec((1, cg), lambda i, g: (0, g))],
        out_specs=pl.BlockSpec((1, tb, cg), out_map),
        out_shape=jax.ShapeDtypeStruct((nb, cl + t, xbc_w), F32),
        compiler_params=pltpu.CompilerParams(dimension_semantics=("arbitrary", "arbitrary"),
                                             vmem_limit_bytes=VMEM_LIMIT),
        name="ssd_prep",
        interpret=interpret,
    )(proj, proj, proj, conv_w, conv_b.reshape(1, xbc_w))


def _mixer_finish_kernel(o_ref, bonus_ref, gate_ref, lnw_ref, lnb_ref,
                         ys_ref, xs_ref, z_ref, dsk_ref, nw_ref, yr_ref, ysd_ref, *, ssd_group):
    o = o_ref[0, 0] + o_ref[1, 0]
    ones = _block_ones(RWKV_HEAD)
    mean = _segsum(o, ones) * (1.0 / RWKV_HEAD)
    cen = o - mean
    var = _segsum(cen * cen, ones) * (1.0 / RWKV_HEAD)
    on = cen * lax.rsqrt(var + RWKV_GN_EPS)
    yr_ref[...] = ((on * lnw_ref[...] + lnb_ref[...] + bonus_ref[0]) * gate_ref[0]).astype(yr_ref.dtype)

    z = z_ref[...]
    y = (ys_ref[0, 0] + ys_ref[1, 0] + dsk_ref[...] * xs_ref[0]) * (z * jax.nn.sigmoid(z))
    sq = y * y
    all_ones = jnp.ones((LANES, LANES), F32)
    tiles = ssd_group // LANES
    parts = []
    for gi in range(y.shape[1] // ssd_group):
        acc = sq[:, gi * ssd_group:gi * ssd_group + LANES]
        for ti in range(1, tiles):
            acc = acc + sq[:, gi * ssd_group + ti * LANES:gi * ssd_group + (ti + 1) * LANES]
        ms = _mm(acc, all_ones, HI) * (1.0 / ssd_group)
        parts.extend([ms] * tiles)
    ms = jnp.concatenate(parts, axis=1)
    ysd_ref[...] = (y * lax.rsqrt(ms + RMS_EPS) * nw_ref[...]).astype(ysd_ref.dtype)


def mixer_finish(o, bonus, gate, ln_w, ln_b, ys, xbc, proj, d_skip, norm_w, *, nb, cl, t, z_col0,
                 ssd_group, cg=512, interpret=False):
    tb = TOK_BLK
    cw = o.shape[-1]
    assert cl % tb == 0 and t % tb == 0 and z_col0 % cg == 0 and cg % ssd_group == 0
    off = cl // tb
    bps = t // tb
    zrow0 = nb * cl // tb
    zc = z_col0 // cg
    two = pl.BlockSpec((2, 1, tb, cg), lambda b, k, g: (0, b, k, g))
    shifted = pl.BlockSpec((1, tb, cg), lambda b, k, g: (b, k + off, g))
    per_c = pl.BlockSpec((1, cg), lambda b, k, g: (0, g))
    out = pl.BlockSpec((tb, cg), lambda b, k, g: (b * bps + k, g))
    return pl.pallas_call(
        functools.partial(_mixer_finish_kernel, ssd_group=ssd_group),
        grid=(nb, bps, cw // cg),
        in_specs=[two, shifted, shifted, per_c, per_c, two, shifted,
                  pl.BlockSpec((tb, cg), lambda b, k, g: (zrow0 + b * bps + k, zc + g)),
                  per_c, per_c],
        out_specs=[out, out],
        out_shape=[jax.ShapeDtypeStruct((nb * t, cw), BF16)] * 2,
        compiler_params=pltpu.CompilerParams(
            dimension_semantics=("arbitrary", "arbitrary", "arbitrary"),
            vmem_limit_bytes=VMEM_LIMIT),
        name="mixer_finish",
        interpret=interpret,
    )(o, bonus, gate, ln_w.reshape(1, cw), ln_b.reshape(1, cw), ys, xbc, proj,
      d_skip.reshape(1, cw), norm_w.reshape(1, cw))


def _col_form(zz, ng):
    b, tt, _, nh = zz.shape
    return jnp.transpose(zz.reshape(b, tt, 2, ng, nh // ng), (2, 0, 3, 1, 4))


def _row_form(zz, ng):
    b, tt, _, nh = zz.shape
    return jnp.transpose(zz.reshape(b, tt, 2, ng, nh // ng), (2, 0, 3, 4, 1))


COL_BLK = 512
LR_BLK = 768


def _by_group(z, ncg):
    rows, _, c = z.shape
    return jnp.transpose(z.reshape(rows, 2, ncg, c // ncg), (0, 2, 1, 3)).reshape(rows, 2 * c)


def _in_proj_layout(w_in, rwkv_mu, w2, a2, g2, w0, a0, cw, sw, xbc_w, nh):
    d = w_in.shape[0]
    rc = rwkv_mu.shape[0]
    lr = rc - 3 * cw
    dr, ir, gr = w2.shape[1], a2.shape[1], g2.shape[0]
    assert lr == 2 * dr + 2 * ir + gr and lr <= LR_BLK and (3 * cw) % LR_BLK == 0
    lr_off = 3 * cw
    z_off = -(-(lr_off + LR_BLK) // COL_BLK) * COL_BLK
    xbc_off = z_off + sw
    dt_off = xbc_off + xbc_w
    total = -(-(dt_off + 2 * nh) // COL_BLK) * COL_BLK
    zc = lambda n: jnp.zeros((d, n), w_in.dtype)
    w_p = jnp.concatenate([
        w_in[:, :rc], zc(z_off - rc), w_in[:, rc:rc + sw + xbc_w + 2 * nh],
        zc(total - dt_off - 2 * nh)], axis=1).astype(BF16)
    mu_w = -(-(lr_off + LR_BLK) // (3 * COL_BLK)) * 3 * COL_BLK
    mu_p = jnp.pad(rwkv_mu, (0, mu_w - rc)).reshape(1, mu_w)
    ncg = cw // COL_BLK
    wd = jnp.zeros((LR_BLK, 2, cw), F32).at[0:dr, 0].set(w2[0]).at[dr:2 * dr, 1].set(w2[1])
    ad = (jnp.zeros((LR_BLK, 2, cw), F32).at[2 * dr:2 * dr + ir, 0].set(a2[0])
          .at[2 * dr + ir:2 * dr + 2 * ir, 1].set(a2[1]))
    gd = jnp.zeros((LR_BLK, cw), F32).at[2 * dr + 2 * ir:lr].set(g2)
    return dict(w=w_p, mu=mu_p, lr_off=lr_off, z_off=z_off, xbc_off=xbc_off, dt_off=dt_off,
                w2g=_by_group(wd, ncg).astype(BF16), a2g=_by_group(ad, ncg).astype(BF16),
                g2g=gd.astype(BF16), w0g=_by_group(w0[None], ncg), a0g=_by_group(a0[None], ncg))


def kernel(x, c, ctx, c_ctx, w_mod, b_mod, norm1_g, norm2_g, w_in, w_out, rwkv_mu, rwkv_w0, rwkv_w2, rwkv_a0, rwkv_a2, rwkv_g2, rwkv_k_k, rwkv_k_a, rwkv_r_k, rwkv_ln_w, rwkv_ln_b, ssm_conv_w, ssm_conv_b, ssm_dt_bias, ssm_a_log, ssm_d, ssm_norm_w, peer_wq, peer_k1, peer_k2, peer_u, peer_v, final_g):
    assert w_in.shape[0] == 1, "single-layer block"
    nb, t, d = x.shape
    cl = ctx.shape[1]
    mod_rows = 8
    s = jnp.concatenate([c, c_ctx[None], jnp.zeros((mod_rows - nb - 1, d), F32)], axis=0)
    mod3 = mod_matmul(s, w_mod[0], b_mod[0]).reshape(mod_rows * N_MOD, 1, d)

    tm = 512
    nctx_blk = nb * cl // tm
    x_blk = t // tm
    h_all = jnp.concatenate([ctx.reshape(nb * cl, d), x.reshape(nb * t, d)], axis=0)
    cw = rwkv_k_k.shape[1]
    sw = ssm_norm_w.shape[1]
    xbc_w = ssm_conv_w.shape[2]
    nh = ssm_a_log.shape[2]
    ng = (xbc_w - sw) // (2 * SSM_STATE)
    lay = _in_proj_layout(w_in[0], rwkv_mu[0], rwkv_w2[0], rwkv_a2[0], rwkv_g2[0], rwkv_w0[0],
                          rwkv_a0[0], cw, sw, xbc_w, nh)
    proj = normmod_matmul(
        h_all, norm1_g[0], mod3, lay["w"],
        row_of_block=lambda i: jnp.where(i < nctx_blk, nb, (i - nctx_blk) // x_blk),
        shift_idx=0, scale_idx=1, n_mod=N_MOD, tm=tm, tn=COL_BLK)

    r, v, kk, ld, kd, bb, bonus, gate = rwkv_prep(
        proj, lay["mu"], lay["w2g"], lay["a2g"], lay["g2g"], lay["w0g"], lay["a0g"],
        rwkv_k_k[0], rwkv_k_a[0], rwkv_r_k[0].reshape(cw), nb=nb, cl=cl, t=t, cw=cw,
        lr_off=lay["lr_off"], lr_w=LR_BLK, grid_w=GRID_W, cg=COL_BLK)
    o = rwkv_scan(r, v, kk, ld, kd, bb, ctx_len=cl)

    xbc = ssd_prep(proj, ssm_conv_w[0], ssm_conv_b[0], nb=nb, cl=cl, t=t, col0=lay["xbc_off"],
                   cg=COL_BLK)
    dt_raw = proj[:, lay["dt_off"]:lay["dt_off"] + 2 * nh]
    dt_raw = jnp.concatenate([dt_raw[:nb * cl].reshape(nb, cl, 2, nh),
                              dt_raw[nb * cl:].reshape(nb, t, 2, nh)], axis=1)
    dt = jax.nn.softplus(dt_raw + ssm_dt_bias[0])
    la = dt * -jnp.exp(ssm_a_log[0])
    ys = ssd_scan(xbc, _col_form(dt, ng), _col_form(la, ng), _row_form(dt, ng), _row_form(la, ng),
                  ctx_len=cl, nst=SSM_STATE)

    y_rwkv, y_ssd = mixer_finish(
        o, bonus, gate, rwkv_ln_w[0], rwkv_ln_b[0], ys, xbc, proj,
        jnp.repeat(ssm_d[0], sw // nh), ssm_norm_w[0], nb=nb, cl=cl, t=t, z_col0=lay["z_off"],
        ssd_group=sw // ng, cg=COL_BLK)
    by_batch = lambda i: i // x_blk
    h1 = matmul_gated_resid(y_rwkv, y_ssd, w_out[0].astype(BF16), x.reshape(nb * t, d), mod3,
                            row_of_block=by_batch, gate_idx=2, n_mod=N_MOD, tm=tm, tn=1024)

    q, xn2 = normmod_matmul(h1, norm2_g[0], mod3, peer_wq[0].astype(BF16), row_of_block=by_batch,
                            shift_idx=3, scale_idx=4, n_mod=N_MOD, tm=tm, tn=512, emit_xn=True)
    s1, s2, e1, e2, tau = peer_scores(q, peer_k1[0], peer_k2[0], topk=PEER_TOPK)
    wt = peer_weights(peer_u[0].astype(BF16), xn2, s1, s2, e1, e2, tau)
    p = peer_out(wt, peer_v[0].astype(BF16))
    tf = 256
    out = final_norm(h1, p, mod3, final_g, row_of_block=lambda i: i // (t // tf), gate_idx=5,
                     n_mod=N_MOD, tm=tf)
    return out.reshape(nb, t, d)
```

```python
import functools
import math

import jax
import jax.numpy as jnp
from jax import lax
from jax.experimental import pallas as pl
from jax.experimental.pallas import tpu as pltpu

F32 = jnp.float32
BF16 = jnp.bfloat16
HI = lax.Precision.HIGHEST

RWKV_CHUNK = 64
RWKV_HEAD = 64
PAIR = 2 * RWKV_HEAD


def _mm(a, b, precision=None):
    return lax.dot_general(a, b, (((1,), (0,)), ((), ())), precision=precision,
                           preferred_element_type=F32)


def _mm_nt(a, b, precision=None):
    return lax.dot_general(a, b, (((1,), (1,)), ((), ())), precision=precision,
                           preferred_element_type=F32)


def _rwkv_masks(is_bwd):
    rho = lax.broadcasted_iota(jnp.int32, (PAIR, PAIR), 0)
    sig = lax.broadcasted_iota(jnp.int32, (PAIR, PAIR), 1)
    same64 = (rho // 64) == (sig // 64)
    same32 = (rho // 32) == (sig // 32)
    same16 = (rho // 16) == (sig // 16)
    sgn = jnp.where(is_bwd, -1, 1).astype(jnp.int32)
    before = ((rho - sig) * sgn > 0) & same64
    eye = rho == sig
    t = lax.broadcasted_iota(jnp.int32, (RWKV_CHUNK, RWKV_CHUNK), 0)
    s = lax.broadcasted_iota(jnp.int32, (RWKV_CHUNK, RWKV_CHUNK), 1)
    tri = jnp.where((t - s) * sgn >= 0, 1.0, 0.0).astype(F32)
    return dict(same64=same64, d16=same16, off32=same32 & ~same16, off64=same64 & ~same32,
                before=before, before_eq=before | eye, eye=eye, tri=tri)


def _each(f, *lists):
    return [f(*xs) for xs in zip(*lists)]


def _rwkv_chunk(r, v, kk, ld, kd, bb, S, m):
    zero = jnp.zeros((), F32)
    bf = lambda z: z.astype(BF16)
    bd, before, eye = m["same64"], m["before"], m["eye"]
    before_eq2 = jnp.concatenate([m["before_eq"]] * 2, axis=1)
    stack2 = lambda x: jnp.where(bd, jnp.concatenate([x, x], axis=0), zero)

    cum = _each(lambda x: _mm(m["tri"], x, HI), ld)
    tot = _each(lambda x: jnp.sum(x, axis=0, keepdims=True), ld)
    e_inv = _each(lambda c: jnp.exp(-c), cum)
    e_end = _each(lambda c, t: jnp.exp(t - c), cum, tot)
    a2 = _each(lambda k, c, x: stack2(-k * jnp.exp(c - x)), kk, cum, ld)
    r2 = _each(lambda x, c: stack2(x * jnp.exp(c)), r, cum)
    b2 = _each(lambda x, e: stack2(x * e), bb, e_inv)
    k2 = _each(lambda x, e: stack2(x * e), kd, e_inv)
    bkg = _each(lambda x, y, e: bf(jnp.concatenate([stack2(x * e), stack2(y * e)], axis=0)),
                bb, kd, e_end)
    v2 = _each(stack2, v)
    sc = _each(lambda a, rr, b, k: _mm_nt(bf(jnp.concatenate([a, rr], axis=0)),
                                          bf(jnp.concatenate([b, k], axis=0))), a2, r2, b2, k2)
    mm = _each(lambda s: jnp.where(before, s[:PAIR, :PAIR], zero), sc)
    nn = _each(lambda s: bf(jnp.where(before, s[:PAIR, PAIR:], zero)), sc)
    pq = _each(lambda s: bf(jnp.where(before_eq2, s[PAIR:], zero)), sc)

    md = _each(lambda x: bf(jnp.where(m["d16"], x, zero)), mm)
    m2 = _each(lambda x: bf(_mm(x, x)), md)
    t_inv = _each(lambda x: jnp.where(eye, 1.0, zero) + x, md)
    tp = _each(lambda t, p: _mm(jnp.concatenate([bf(t), p], axis=0), p), t_inv, m2)
    t_inv = _each(lambda t, x: t + x[:PAIR], t_inv, tp)
    m4 = _each(lambda x: bf(x[PAIR:]), tp)
    tp = _each(lambda t, p: _mm(jnp.concatenate([bf(t), p], axis=0), p), t_inv, m4)
    t_inv = _each(lambda t, x: t + x[:PAIR], t_inv, tp)
    m8 = _each(lambda x: bf(x[PAIR:]), tp)
    t_inv = _each(lambda t, p: t + _mm(bf(t), p), t_inv, m8)
    for off in ("off32", "off64"):
        mo = _each(lambda x: bf(jnp.where(m[off], x, zero)), mm)
        tb = _each(bf, t_inv)
        mt = _each(lambda a, b: bf(_mm(a, b)), mo, tb)
        t_inv = _each(lambda t, a, b: t + _mm(a, b), t_inv, tb, mt)
    tb = _each(bf, t_inv)

    at = _each(lambda x: x.T, a2)
    vt = _each(lambda x: bf(x.T), v2)
    rt = _each(lambda x: x.T, r2)
    nvt = _each(_mm_nt, vt, nn)
    x1 = _each(lambda a, n, t: _mm_nt(bf(jnp.concatenate([a, n], axis=0)), t), at, nvt, tb)
    atp = _each(lambda x: bf(x[:PAIR]), x1)
    cv = _each(lambda x, y: jnp.concatenate([bf(x[PAIR:]), y], axis=1), x1, vt)
    g = _each(lambda a, b, t: _mm(a, b[:PAIR]) + jnp.where(eye, jnp.exp(t), zero), atp, bkg, tot)
    rtp = _each(lambda x, a, p: x + _mm_nt(a, p[:, :PAIR]), rt, atp, pq)
    c2 = _each(_mm, cv, bkg)
    c3 = _each(_mm_nt, cv, pq)
    so = _each(lambda s, a, b: _mm(bf(s), bf(jnp.concatenate([a, b], axis=1))), S, g, rtp)
    s_new = _each(lambda x, c: x[:, :PAIR] + c, so, c2)
    o2 = _each(lambda x, c: (x[:, PAIR:] + c).T, so, c3)
    return _each(lambda x: x[:RWKV_CHUNK] + x[RWKV_CHUNK:], o2), s_new


def _rwkv_kernel(r_ref, v_ref, kk_ref, ld_ref, kd_ref, bb_ref, o_ref, s_ref, *, npair, nch):
    is_bwd = pl.program_id(0) == 1
    blk = pl.program_id(3)

    @pl.when(blk == 0)
    def _():
        s_ref[...] = jnp.zeros_like(s_ref)

    m = _rwkv_masks(is_bwd)
    lanes = [slice(p * PAIR, (p + 1) * PAIR) for p in range(npair)]

    def chunk(c, carry):
        ci = jnp.where(is_bwd, nch - 1 - c, c)
        rows = pl.ds(pl.multiple_of(ci * RWKV_CHUNK, RWKV_CHUNK), RWKV_CHUNK)
        o, s_new = _rwkv_chunk(
            [r_ref[0, rows, ln] for ln in lanes], [v_ref[0, rows, ln] for ln in lanes],
            [kk_ref[0, rows, ln] for ln in lanes], [ld_ref[0, 0, rows, ln] for ln in lanes],
            [kd_ref[0, 0, rows, ln] for ln in lanes], [bb_ref[0, 0, rows, ln] for ln in lanes],
            [s_ref[p] for p in range(npair)], m)
        for p in range(npair):
            s_ref[p] = s_new[p]
            o_ref[0, 0, rows, lanes[p]] = o[p]
        return carry

    lax.fori_loop(0, nch, chunk, 0)


def rwkv_scan(r, v, kk, ld, kd, bb, *, ctx_len, tb=256, lanes=1024, interpret=False):
    b, tt, c = r.shape
    assert ctx_len % tb == 0 and tt % tb == 0 and c % lanes == 0 and lanes % PAIR == 0
    nctx = ctx_len // tb
    nblk = tt // tb

    def tmap(d, k):
        fwd = k
        bwd = jnp.where(k < nctx, nctx - 1 - k, nblk - 1 - (k - nctx))
        return jnp.where(d == 1, bwd, fwd)

    shared = pl.BlockSpec((1, tb, lanes), lambda d, bi, h, k: (bi, tmap(d, k), h))
    direc = pl.BlockSpec((1, 1, tb, lanes), lambda d, bi, h, k: (d, bi, tmap(d, k), h))
    out = pl.BlockSpec((1, 1, tb, lanes),
                       lambda d, bi, h, k: (d, bi, tmap(d, jnp.maximum(k, nctx)) - nctx, h))
    npair = lanes // PAIR
    return pl.pallas_call(
        functools.partial(_rwkv_kernel, npair=npair, nch=tb // RWKV_CHUNK),
        grid=(2, b, c // lanes, nblk),
        in_specs=[shared, shared, shared, direc, direc, direc],
        out_specs=out,
        out_shape=jax.ShapeDtypeStruct((2, b, tt - ctx_len, c), F32),
        scratch_shapes=[pltpu.VMEM((npair, PAIR, PAIR), F32)],
        compiler_params=pltpu.CompilerParams(
            dimension_semantics=("arbitrary", "arbitrary", "arbitrary", "arbitrary")),
        name="rwkv_scan",
        interpret=interpret,
    )(r, v, kk, ld, kd, bb)


def _time_block_map(nctx, nblk):
    def tmap(d, k):
        bwd = jnp.where(k < nctx, nctx - 1 - k, nblk - 1 - (k - nctx))
        return jnp.where(d == 1, bwd, k)
    return tmap


SSD_P = 64


def _ssd_kernel(xs_ref, bm_ref, cm_ref, dtc_ref, lac_ref, dtr_ref, lar_ref, y_ref, st_ref, *, chunk):
    is_bwd = pl.program_id(0) == 1
    blk = pl.program_id(3)

    @pl.when(blk == 0)
    def _():
        st_ref[...] = jnp.zeros_like(st_ref)

    sgn = jnp.where(is_bwd, -1, 1).astype(jnp.int32)
    l_i = lax.broadcasted_iota(jnp.int32, (chunk, chunk), 0)
    s_i = lax.broadcasted_iota(jnp.int32, (chunk, chunk), 1)
    incl = (l_i - s_i) * sgn >= 0
    tri = jnp.where(incl, 1.0, 0.0).astype(F32)
    lane = lax.broadcasted_iota(jnp.int32, (chunk, PAIR), 1)
    first = lane < SSD_P
    first_row = first[:1]

    ngb = dtc_ref.shape[2]
    nh = dtc_ref.shape[-1]
    nst = bm_ref.shape[-1] // ngb
    groups = list(range(ngb))
    pairs = [(gi, q) for gi in groups for q in range(nh // 2)]
    heads = [(gi, h) for gi in groups for h in range(nh)]
    pair_lanes = {(gi, q): slice((gi * nh // 2 + q) * PAIR, (gi * nh // 2 + q + 1) * PAIR)
                  for gi, q in pairs}

    bmat = [bm_ref[0, :, gi * nst:(gi + 1) * nst] for gi in groups]
    cmat_b = [cm_ref[0, :, gi * nst:(gi + 1) * nst].astype(BF16) for gi in groups]
    dtc = [dtc_ref[0, 0, gi] for gi in groups]
    lac = [lac_ref[0, 0, gi] for gi in groups]
    dtr = [dtr_ref[0, 0, gi] for gi in groups]
    lar = [lar_ref[0, 0, gi] for gi in groups]

    cum_col = _each(lambda z: _mm(tri, z, HI), lac)
    cum_row = _each(lambda z: _mm_nt(z, tri, HI), lar)
    cb = _each(lambda c, b: _mm_nt(c, b.astype(BF16)), cmat_b, bmat)
    b_t = _each(lambda b: b.T.astype(BF16), bmat)
    tot = _each(lambda z: jnp.sum(z, axis=0, keepdims=True), lac)
    coef = _each(lambda t, c, z: jnp.exp(t - c) * z, tot, cum_col, dtc)
    ecum = _each(jnp.exp, cum_col)
    etot = _each(jnp.exp, tot)

    xp = [xs_ref[0, :, pair_lanes[p]] for p in pairs]
    xp_b = _each(lambda z: z.astype(BF16), xp)
    st = [st_ref[:, pair_lanes[p]] for p in pairs]
    wmat = [(cb[gi] * jnp.exp(jnp.where(incl, cum_col[gi][:, h:h + 1] - cum_row[gi][h:h + 1, :],
                                        -jnp.inf)) * dtr[gi][h:h + 1, :]).astype(BF16)
            for gi, h in heads]
    yh = [_mm(w, xp_b[i // 2]) for i, w in enumerate(wmat)]
    yoff = [_mm(cmat_b[gi], s.astype(BF16)) for (gi, q), s in zip(pairs, st)]
    snew = [_mm(b_t[gi], (x * jnp.where(first, coef[gi][:, 2 * q:2 * q + 1],
                                        coef[gi][:, 2 * q + 1:2 * q + 2])).astype(BF16))
            for (gi, q), x in zip(pairs, xp)]
    for i, (gi, q) in enumerate(pairs):
        h0, h1 = 2 * q, 2 * q + 1
        y = jnp.where(first, yh[2 * i], yh[2 * i + 1])
        y = y + yoff[i] * jnp.where(first, ecum[gi][:, h0:h0 + 1], ecum[gi][:, h1:h1 + 1])
        st_ref[:, pair_lanes[(gi, q)]] = (
            st[i] * jnp.where(first_row, etot[gi][:, h0:h0 + 1], etot[gi][:, h1:h1 + 1]) + snew[i])
        y_ref[0, 0, :, pair_lanes[(gi, q)]] = y


def ssd_scan(xbc, dtc, lac, dtr, lar, *, ctx_len, nst, chunk=256, ngb=2, interpret=False):
    b, tt, _ = xbc.shape
    ng, nh = dtc.shape[2], dtc.shape[4]
    gw = ngb * nh * SSD_P
    width = ng * nh * SSD_P
    bw = ngb * nst
    assert xbc.shape[2] == width + 2 * ng * nst and width % bw == 0 and ng % ngb == 0
    assert ctx_len % chunk == 0 and tt % chunk == 0 and nh % 2 == 0
    nctx, nblk = ctx_len // chunk, tt // chunk
    tmap = _time_block_map(nctx, nblk)
    b0 = width // bw
    c0 = b0 + ng // ngb
    return pl.pallas_call(
        functools.partial(_ssd_kernel, chunk=chunk),
        grid=(2, b, ng // ngb, nblk),
        in_specs=[
            pl.BlockSpec((1, chunk, gw), lambda d, bi, g, k: (bi, tmap(d, k), g)),
            pl.BlockSpec((1, chunk, bw), lambda d, bi, g, k: (bi, tmap(d, k), b0 + g)),
            pl.BlockSpec((1, chunk, bw), lambda d, bi, g, k: (bi, tmap(d, k), c0 + g)),
            pl.BlockSpec((1, 1, ngb, chunk, nh), lambda d, bi, g, k: (d, bi, g, tmap(d, k), 0)),
            pl.BlockSpec((1, 1, ngb, chunk, nh), lambda d, bi, g, k: (d, bi, g, tmap(d, k), 0)),
            pl.BlockSpec((1, 1, ngb, nh, chunk), lambda d, bi, g, k: (d, bi, g, 0, tmap(d, k))),
            pl.BlockSpec((1, 1, ngb, nh, chunk), lambda d, bi, g, k: (d, bi, g, 0, tmap(d, k))),
        ],
        out_specs=pl.BlockSpec(
            (1, 1, chunk, gw),
            lambda d, bi, g, k: (d, bi, tmap(d, jnp.maximum(k, nctx)) - nctx, g)),
        out_shape=jax.ShapeDtypeStruct((2, b, tt - ctx_len, width), F32),
        scratch_shapes=[pltpu.VMEM((nst, gw), F32)],
        compiler_params=pltpu.CompilerParams(
            dimension_semantics=("arbitrary", "arbitrary", "arbitrary", "arbitrary")),
        name="ssd_scan",
        interpret=interpret,
    )(xbc, xbc, xbc, dtc, lac, dtr, lar)


RMS_EPS = 1e-6
VMEM_LIMIT = 56 * 1024 * 1024


def _mod_kernel(s_ref, w_ref, b_ref, o_ref):
    s = s_ref[...]
    s = s * jax.nn.sigmoid(s)
    o_ref[...] = _mm(s, w_ref[...], HI) + b_ref[...]


def mod_matmul(s, w, bias, *, tn=512, interpret=False):
    m, k = s.shape
    n = w.shape[1]
    return pl.pallas_call(
        _mod_kernel,
        grid=(n // tn,),
        in_specs=[pl.BlockSpec((m, k), lambda j: (0, 0)),
                  pl.BlockSpec((k, tn), lambda j: (0, j)),
                  pl.BlockSpec((1, tn), lambda j: (0, j))],
        out_specs=pl.BlockSpec((m, tn), lambda j: (0, j)),
        out_shape=jax.ShapeDtypeStruct((m, n), F32),
        compiler_params=pltpu.CompilerParams(dimension_semantics=("arbitrary",),
                                             vmem_limit_bytes=VMEM_LIMIT),
        name="mod_matmul",
        interpret=interpret,
    )(s, w, bias.reshape(1, n))


def _normmod_matmul_kernel(h_ref, g_ref, sh_ref, sc_ref, w_ref, o_ref, *rest, emit_xn):
    xn_ref = rest[-1]

    @pl.when(pl.program_id(1) == 0)
    def _():
        h = h_ref[...]
        hn = h * lax.rsqrt(jnp.mean(h * h, axis=-1, keepdims=True) + RMS_EPS)
        xn = (hn * g_ref[...]) * (1.0 + sc_ref[0]) + sh_ref[0]
        xn_ref[...] = xn.astype(BF16)
        if emit_xn:
            rest[0][...] = xn.astype(BF16)

    o_ref[...] = _mm(xn_ref[...], w_ref[...])


def normmod_matmul(h, g, mod3, w, *, row_of_block, shift_idx, scale_idx, n_mod, tm, tn,
                   emit_xn=False, interpret=False):
    rows, d = h.shape
    n = w.shape[1]
    assert rows % tm == 0 and n % tn == 0
    out_shape = [jax.ShapeDtypeStruct((rows, n), F32)]
    out_specs = [pl.BlockSpec((tm, tn), lambda i, j: (i, j))]
    if emit_xn:
        out_shape.append(jax.ShapeDtypeStruct((rows, d), BF16))
        out_specs.append(pl.BlockSpec((tm, d), lambda i, j: (i, 0)))
    res = pl.pallas_call(
        functools.partial(_normmod_matmul_kernel, emit_xn=emit_xn),
        grid=(rows // tm, n // tn),
        in_specs=[
            pl.BlockSpec((tm, d), lambda i, j: (i, 0)),
            pl.BlockSpec((1, d), lambda i, j: (0, 0)),
            pl.BlockSpec((1, 1, d), lambda i, j: (row_of_block(i) * n_mod + shift_idx, 0, 0)),
            pl.BlockSpec((1, 1, d), lambda i, j: (row_of_block(i) * n_mod + scale_idx, 0, 0)),
            pl.BlockSpec((d, tn), lambda i, j: (0, j)),
        ],
        out_specs=out_specs,
        out_shape=out_shape,
        scratch_shapes=[pltpu.VMEM((tm, d), BF16)],
        compiler_params=pltpu.CompilerParams(dimension_semantics=("arbitrary", "arbitrary"),
                                             vmem_limit_bytes=VMEM_LIMIT),
        name="normmod_matmul",
        interpret=interpret,
    )(h, g.reshape(1, d), mod3, mod3, w)
    return res if emit_xn else res[0]


def _matmul_gated_resid_kernel(ya_ref, yb_ref, w_ref, x_ref, gt_ref, o_ref):
    ka = ya_ref.shape[1]
    acc = _mm(ya_ref[...], w_ref[:ka, :]) + _mm(yb_ref[...], w_ref[ka:, :])
    o_ref[...] = x_ref[...] + gt_ref[0] * acc


def matmul_gated_resid(ya, yb, w, resid, mod3, *, row_of_block, gate_idx, n_mod, tm, tn,
                       interpret=False):
    rows, ka = ya.shape
    kb = yb.shape[1]
    k, n = w.shape
    assert rows % tm == 0 and n % tn == 0 and ka + kb == k
    return pl.pallas_call(
        _matmul_gated_resid_kernel,
        grid=(rows // tm, n // tn),
        in_specs=[
            pl.BlockSpec((tm, ka), lambda i, j: (i, 0)),
            pl.BlockSpec((tm, kb), lambda i, j: (i, 0)),
            pl.BlockSpec((k, tn), lambda i, j: (0, j)),
            pl.BlockSpec((tm, tn), lambda i, j: (i, j)),
            pl.BlockSpec((1, 1, tn), lambda i, j: (row_of_block(i) * n_mod + gate_idx, 0, j)),
        ],
        out_specs=pl.BlockSpec((tm, tn), lambda i, j: (i, j)),
        out_shape=jax.ShapeDtypeStruct((rows, n), F32),
        compiler_params=pltpu.CompilerParams(dimension_semantics=("arbitrary", "arbitrary"),
                                             vmem_limit_bytes=VMEM_LIMIT),
        name="matmul_gated_resid",
        interpret=interpret,
    )(ya, yb, w, resid, mod3)


def _top_values(s, k):
    n = s.shape[0]
    row = lax.broadcasted_iota(jnp.int32, s.shape, 0)
    vals = []
    for _ in range(k):
        mx = jnp.max(s, axis=0, keepdims=True)
        first = jnp.min(jnp.where(s == mx, row, n), axis=0, keepdims=True)
        s = jnp.where(row == first, -jnp.inf, s)
        vals.append(mx)
    return jnp.concatenate(vals, axis=0)


def _peer_score_kernel(q_ref, k1_ref, k2_ref, s1_ref, s2_ref, e1_ref, e2_ref, tau_ref, *, topk):
    nheads, nkeys, qd = k1_ref.shape
    taus = []
    for h in range(nheads):
        q1 = q_ref[:, h * 2 * qd:(h * 2 + 1) * qd]
        q2 = q_ref[:, (h * 2 + 1) * qd:(h * 2 + 2) * qd]
        s1 = _mm_nt(k1_ref[h], q1, HI)
        s2 = _mm_nt(k2_ref[h], q2, HI)
        v1 = _top_values(s1, topk)
        v2 = _top_values(s2, topk)
        cand = jnp.concatenate([v1[a:a + 1] + v2[:topk // (a + 1)] for a in range(topk)], axis=0)
        best = _top_values(cand, topk)
        z = jnp.sum(jnp.exp(best - best[:1]), axis=0, keepdims=True)
        s1_ref[h] = s1
        s2_ref[h] = s2
        e1_ref[h] = jnp.exp(s1 - v1[:1])
        e2_ref[h] = jnp.exp(s2 - v2[:1]) / z
        taus.append(best[topk - 1:topk])
    tau_ref[...] = jnp.concatenate(taus, axis=0)


def peer_scores(q, k1, k2, *, topk, tb=256, interpret=False):
    t = q.shape[0]
    nheads, nkeys, qd = k1.shape
    big = pl.BlockSpec((nheads, nkeys, tb), lambda i: (0, 0, i))
    big_shape = jax.ShapeDtypeStruct((nheads, nkeys, t), F32)
    return pl.pallas_call(
        functools.partial(_peer_score_kernel, topk=topk),
        grid=(t // tb,),
        in_specs=[pl.BlockSpec((tb, q.shape[1]), lambda i: (i, 0)),
                  pl.BlockSpec(k1.shape, lambda i: (0, 0, 0)),
                  pl.BlockSpec(k2.shape, lambda i: (0, 0, 0))],
        out_specs=[big, big, big, big, pl.BlockSpec((nheads, tb), lambda i: (0, i))],
        out_shape=[big_shape] * 4 + [jax.ShapeDtypeStruct((nheads, t), F32)],
        compiler_params=pltpu.CompilerParams(dimension_semantics=("arbitrary",),
                                             vmem_limit_bytes=VMEM_LIMIT),
        name="peer_scores",
        interpret=interpret,
    )(q, k1, k2)


def _peer_weight_kernel(u_ref, x_ref, s1_ref, s2_ref, e1_ref, e2_ref, tau_ref, w_ref):
    nheads, nkeys, _ = s2_ref.shape
    a = _mm_nt(u_ref[...], x_ref[...])
    act = 0.5 * a * (1.0 + lax.erf(a * (1.0 / math.sqrt(2.0))))
    for ii in range(u_ref.shape[0] // nkeys):
        gate = jnp.zeros((nkeys, x_ref.shape[0]), F32)
        for h in range(nheads):
            pair = s1_ref[h, ii:ii + 1, :] + s2_ref[h]
            gate = gate + jnp.where(pair >= tau_ref[h:h + 1, :],
                                    e1_ref[h, ii:ii + 1, :] * e2_ref[h], 0.0)
        rows = slice(ii * nkeys, (ii + 1) * nkeys)
        w_ref[rows, :] = (act[rows] * gate).astype(w_ref.dtype)


def peer_weights(u, xn, s1, s2, e1, e2, tau, *, te=1024, tb=512, interpret=False):
    ne, d = u.shape
    t = xn.shape[0]
    nheads, nkeys, _ = s1.shape
    ni = te // nkeys
    s1_spec = pl.BlockSpec((nheads, ni, tb), lambda i, j: (0, j, i))
    s2_spec = pl.BlockSpec((nheads, nkeys, tb), lambda i, j: (0, 0, i))
    return pl.pallas_call(
        _peer_weight_kernel,
        grid=(t // tb, ne // te),
        in_specs=[pl.BlockSpec((te, d), lambda i, j: (j, 0)),
                  pl.BlockSpec((tb, d), lambda i, j: (i, 0)),
                  s1_spec, s2_spec, s1_spec, s2_spec,
                  pl.BlockSpec((nheads, tb), lambda i, j: (0, i))],
        out_specs=pl.BlockSpec((te, tb), lambda i, j: (j, i)),
        out_shape=jax.ShapeDtypeStruct((ne, t), BF16),
        compiler_params=pltpu.CompilerParams(dimension_semantics=("arbitrary", "arbitrary"),
                                             vmem_limit_bytes=VMEM_LIMIT),
        name="peer_weights",
        interpret=interpret,
    )(u, xn, s1, s2, e1, e2, tau)


def _peer_out_kernel(w_ref, v_ref, o_ref):
    @pl.when(pl.program_id(1) == 0)
    def _():
        o_ref[...] = jnp.zeros_like(o_ref)

    o_ref[...] += lax.dot_general(w_ref[...], v_ref[...], (((0,), (0,)), ((), ())),
                                  preferred_element_type=F32)


def peer_out(wt, v, *, tb=512, te=1024, interpret=False):
    ne, t = wt.shape
    d = v.shape[1]
    return pl.pallas_call(
        _peer_out_kernel,
        grid=(t // tb, ne // te),
        in_specs=[pl.BlockSpec((te, tb), lambda i, j: (j, i)),
                  pl.BlockSpec((te, d), lambda i, j: (j, 0))],
        out_specs=pl.BlockSpec((tb, d), lambda i, j: (i, 0)),
        out_shape=jax.ShapeDtypeStruct((t, d), F32),
        compiler_params=pltpu.CompilerParams(dimension_semantics=("arbitrary", "arbitrary"),
                                             vmem_limit_bytes=VMEM_LIMIT),
        name="peer_out",
        interpret=interpret,
    )(wt, v)


def _final_kernel(h_ref, p_ref, gt_ref, g_ref, o_ref):
    h = h_ref[...] + gt_ref[0] * p_ref[...]
    o_ref[...] = h * lax.rsqrt(jnp.mean(h * h, axis=-1, keepdims=True) + RMS_EPS) * g_ref[...]


def final_norm(h, p, mod3, g, *, row_of_block, gate_idx, n_mod, tm=256, interpret=False):
    rows, d = h.shape
    blk = pl.BlockSpec((tm, d), lambda i: (i, 0))
    return pl.pallas_call(
        _final_kernel,
        grid=(rows // tm,),
        in_specs=[blk, blk,
                  pl.BlockSpec((1, 1, d), lambda i: (row_of_block(i) * n_mod + gate_idx, 0, 0)),
                  pl.BlockSpec((1, d), lambda i: (0, 0))],
        out_specs=blk,
        out_shape=jax.ShapeDtypeStruct((rows, d), F32),
        compiler_params=pltpu.CompilerParams(dimension_semantics=("arbitrary",),
                                             vmem_limit_bytes=VMEM_LIMIT),
        name="final_norm",
        interpret=interpret,
    )(h, p, mod3, g.reshape(1, d))


N_MOD = 6
GRID_W = 64
RWKV_GN_EPS = 64e-5
SSM_STATE = 128
PEER_TOPK = 16


LANES = 128
TOK_BLK = 256


def _block_ones(group):
    i = lax.broadcasted_iota(jnp.int32, (LANES, LANES), 0) // group
    j = lax.broadcasted_iota(jnp.int32, (LANES, LANES), 1) // group
    return jnp.where(i == j, 1.0, 0.0).astype(BF16)


def _ones_mm(x, ones):
    hi = x.astype(BF16)
    lo = (x - hi.astype(F32)).astype(BF16)
    return _mm(hi, ones) + _mm(lo, ones)


def _segsum(x, ones):
    return jnp.concatenate([_ones_mm(x[:, i * LANES:(i + 1) * LANES], ones)
                            for i in range(x.shape[1] // LANES)], axis=1)


def _softplus(x):
    return jnp.maximum(x, 0.0) + jnp.log(1.0 + jnp.exp(-jnp.abs(x)))


def _shift_mix(cur_ref, prev_ref, next_ref, mu_ref, is_ctx, top, bottom, grid_w):
    cur = cur_ref[...]
    n, w = cur.shape
    row = lax.broadcasted_iota(jnp.int32, (n, w), 0)
    lane = lax.broadcasted_iota(jnp.int32, (n, w), 1)
    before = pltpu.roll(cur, 1, axis=0)
    after = pltpu.roll(cur, n - 1, axis=0)
    col = row % grid_w
    left = jnp.where(col == 0, 0.0, before)
    right = jnp.where(col == grid_w - 1, 0.0, after)
    up = jnp.concatenate([prev_ref[...] * (1.0 - top), cur[:n - grid_w]], axis=0)
    down = jnp.concatenate([cur[grid_w:], next_ref[...] * (1.0 - bottom)], axis=0)
    slot = lane % 4
    sh_grid = jnp.where(slot == 0, left, jnp.where(slot == 1, right, jnp.where(slot == 2, up, down)))
    sh_seq = jnp.where(lane % 2 == 0, jnp.where(row == 0, 0.0, before),
                       jnp.where(row == n - 1, 0.0, after))
    shifted = jnp.where(is_ctx, sh_seq, sh_grid)
    return cur + mu_ref[...] * (shifted - cur)


def _rwkv_prep_kernel(r_c, r_p, r_n, k_c, k_p, k_n, v_c, v_p, v_n, l_c, l_p, l_n,
                      mu_r, mu_k, mu_v, mu_l, w2_ref, a2_ref, g2_ref, w0_ref, a0_ref,
                      kk_ref, ka_ref, rk_ref,
                      r_o, v_o, kkn_o, ld_o, kd_o, bb_o, bonus_o, gate_o,
                      *, nctx_blk, blk_per_seq, grid_w):
    i = pl.program_id(0)
    is_ctx = i < nctx_blk
    tblk = (i - nctx_blk) % blk_per_seq
    top = jnp.where(tblk == 0, 1.0, 0.0)
    bottom = jnp.where(tblk == blk_per_seq - 1, 1.0, 0.0)
    mix = functools.partial(_shift_mix, is_ctx=is_ctx, top=top, bottom=bottom, grid_w=grid_w)
    r = mix(r_c, r_p, r_n, mu_r)
    k = mix(k_c, k_p, k_n, mu_k)
    v = mix(v_c, v_p, v_n, mu_v)
    low = mix(l_c, l_p, l_n, mu_l)
    cw = r.shape[1]

    w_low = _mm(jnp.tanh(low).astype(BF16), w2_ref[...])
    a_low = _mm(low.astype(BF16), a2_ref[...])
    gate_o[0] = _mm(jax.nn.sigmoid(low).astype(BF16), g2_ref[...])

    ones = _block_ones(RWKV_HEAD)
    kraw = k * kk_ref[...]
    kkn = kraw * lax.rsqrt(jnp.maximum(_segsum(kraw * kraw, ones), 1e-24))
    ksum = jnp.zeros_like(k)
    for d in range(2):
        cols = slice(d * cw, (d + 1) * cw)
        w = -_softplus(-(w0_ref[:, cols] + w_low[:, cols])) - 0.5
        a = jax.nn.sigmoid(a0_ref[:, cols] + a_low[:, cols])
        k_dir = k * (1.0 + (a - 1.0) * ka_ref[...])
        ld_o[d, 0] = -jnp.exp(w)
        kd_o[d, 0] = k_dir
        bb_o[d, 0] = kkn * a
        ksum = ksum + k_dir
    r_o[0] = r
    v_o[0] = v
    kkn_o[0] = kkn
    bonus_o[0] = _segsum(r * ksum * rk_ref[...], ones) * v


def rwkv_prep(proj, mu_p, w2g, a2g, g2g, w0g, a0g, k_k, k_a, r_k, *, nb, cl, t, cw, lr_off, lr_w,
              grid_w, cg=512, interpret=False):
    tb = TOK_BLK
    assert cl == tb and t % tb == 0 and tb % grid_w == 0 and cw % cg == 0 and lr_off % lr_w == 0
    nctx_blk = nb
    blk_per_seq = t // tb
    nblk = nctx_blk + nb * blk_per_seq
    hb = tb // grid_w
    last_h = nblk * hb - 1
    ncg = cw // cg

    def seg(col0, width):
        cb = col0 // width
        return [pl.BlockSpec((tb, width), lambda i, g, cb=cb, s=(width == cg): (i, cb + g * s)),
                pl.BlockSpec((grid_w, width),
                             lambda i, g, cb=cb, s=(width == cg): (jnp.maximum(i * hb - 1, 0), cb + g * s)),
                pl.BlockSpec((grid_w, width),
                             lambda i, g, cb=cb, s=(width == cg): (jnp.minimum((i + 1) * hb, last_h), cb + g * s))]

    def mu_spec(col0, width):
        cb = col0 // width
        return pl.BlockSpec((1, width), lambda i, g, cb=cb, s=(width == cg): (0, cb + g * s))

    def tok(i):
        b = jnp.where(i < nctx_blk, i, (i - nctx_blk) // blk_per_seq)
        tk = jnp.where(i < nctx_blk, 0, 1 + (i - nctx_blk) % blk_per_seq)
        return b, tk

    def out1(i, g):
        b, tk = tok(i)
        return (b, tk, g)

    def out2(i, g):
        b, tk = tok(i)
        return (0, b, tk, g)

    per_c = pl.BlockSpec((1, cg), lambda i, g: (0, g))
    one_shape = jax.ShapeDtypeStruct((nb, cl + t, cw), F32)
    two_shape = jax.ShapeDtypeStruct((2, nb, cl + t, cw), F32)
    one_spec = pl.BlockSpec((1, tb, cg), out1)
    two_spec = pl.BlockSpec((2, 1, tb, cg), out2)
    return pl.pallas_call(
        functools.partial(_rwkv_prep_kernel, nctx_blk=nctx_blk, blk_per_seq=blk_per_seq,
                          grid_w=grid_w),
        grid=(nblk, ncg),
        in_specs=(seg(0, cg) + seg(cw, cg) + seg(2 * cw, cg) + seg(lr_off, lr_w)
                  + [mu_spec(0, cg), mu_spec(cw, cg), mu_spec(2 * cw, cg), mu_spec(lr_off, lr_w)]
                  + [pl.BlockSpec((lr_w, 2 * cg), lambda i, g: (0, g)),
                     pl.BlockSpec((lr_w, 2 * cg), lambda i, g: (0, g)),
                     pl.BlockSpec((lr_w, cg), lambda i, g: (0, g)),
                     pl.BlockSpec((1, 2 * cg), lambda i, g: (0, g)),
                     pl.BlockSpec((1, 2 * cg), lambda i, g: (0, g)),
                     per_c, per_c, per_c]),
        out_specs=[one_spec, one_spec, one_spec, two_spec, two_spec, two_spec, one_spec, one_spec],
        out_shape=[one_shape] * 3 + [two_shape] * 3 + [one_shape] * 2,
        compiler_params=pltpu.CompilerParams(dimension_semantics=("arbitrary", "arbitrary"),
                                             vmem_limit_bytes=VMEM_LIMIT),
        name="rwkv_prep",
        interpret=interpret,
    )(*([proj] * 12), mu_p, mu_p, mu_p, mu_p, w2g, a2g, g2g, w0g, a0g,
      k_k.reshape(1, cw), k_a.reshape(1, cw), r_k.reshape(1, cw))


def _ssd_prep_kernel(c_ref, p_ref, n_ref, w_ref, b_ref, o_ref, *, nctx_blk, blk_per_seq):
    i = pl.program_id(0)
    is_ctx = i < nctx_blk
    tblk = (i - nctx_blk) % blk_per_seq
    first = jnp.logical_or(is_ctx, tblk == 0)
    last = jnp.logical_or(is_ctx, tblk == blk_per_seq - 1)
    cur = c_ref[...]
    n = cur.shape[0]
    row = lax.broadcasted_iota(jnp.int32, cur.shape, 0)
    halo = p_ref.shape[0]
    prev_row = p_ref[halo - 1:halo, :] * jnp.where(first, 0.0, 1.0)
    next_row = n_ref[0:1, :] * jnp.where(last, 0.0, 1.0)
    before = jnp.where(row == 0, prev_row, pltpu.roll(cur, 1, axis=0))
    after = jnp.where(row == n - 1, next_row, pltpu.roll(cur, n - 1, axis=0))
    y = w_ref[0:1, :] * before + w_ref[1:2, :] * cur + w_ref[2:3, :] * after + b_ref[...]
    o_ref[0] = y * jax.nn.sigmoid(y)


def ssd_prep(proj, conv_w, conv_b, *, nb, cl, t, col0, cg=512, interpret=False):
    tb = TOK_BLK
    halo = 8
    xbc_w = conv_w.shape[1]
    assert conv_w.shape[0] == 3 and cl == tb and t % tb == 0 and col0 % cg == 0 and xbc_w % cg == 0
    nctx_blk = nb
    blk_per_seq = t // tb
    nblk = nctx_blk + nb * blk_per_seq
    hb = tb // halo
    last_h = nblk * hb - 1
    cb = col0 // cg

    def out_map(i, g):
        b = jnp.where(i < nctx_blk, i, (i - nctx_blk) // blk_per_seq)
        tk = jnp.where(i < nctx_blk, 0, 1 + (i - nctx_blk) % blk_per_seq)
        return (b, tk, g)

    return pl.pallas_call(
        functools.partial(_ssd_prep_kernel, nctx_blk=nctx_blk, blk_per_seq=blk_per_seq),
        grid=(nblk, xbc_w // cg),
        in_specs=[pl.BlockSpec((tb, cg), lambda i, g: (i, cb + g)),
                  pl.BlockSpec((halo, cg), lambda i, g: (jnp.maximum(i * hb - 1, 0), cb + g)),
                  pl.BlockSpec((halo, cg), lambda i, g: (jnp.minimum((i + 1) * hb, last_h), cb + g)),
                  pl.BlockSpec((3, cg), lambda i, g: (0, g)),
                  pl.BlockSpec((1, cg), lambda i, g: (0, g))],
        out_specs=pl.BlockSpec((1, tb, cg), out_map),
        out_shape=jax.ShapeDtypeStruct((nb, cl + t, xbc_w), F32),
        compiler_params=pltpu.CompilerParams(dimension_semantics=("arbitrary", "arbitrary"),
                                             vmem_limit_bytes=VMEM_LIMIT),
        name="ssd_prep",
        interpret=interpret,
    )(proj, proj, proj, conv_w, conv_b.reshape(1, xbc_w))


def _mixer_finish_kernel(o_ref, bonus_ref, gate_ref, lnw_ref, lnb_ref,
                         ys_ref, xs_ref, z_ref, dsk_ref, nw_ref, yr_ref, ysd_ref, *, ssd_group):
    o = o_ref[0, 0] + o_ref[1, 0]
    ones = _block_ones(RWKV_HEAD)
    mean = _segsum(o, ones) * (1.0 / RWKV_HEAD)
    cen = o - mean
    var = _segsum(cen * cen, ones) * (1.0 / RWKV_HEAD)
    on = cen * lax.rsqrt(var + RWKV_GN_EPS)
    yr_ref[...] = ((on * lnw_ref[...] + lnb_ref[...] + bonus_ref[0]) * gate_ref[0]).astype(yr_ref.dtype)

    z = z_ref[...]
    y = (ys_ref[0, 0] + ys_ref[1, 0] + dsk_ref[...] * xs_ref[0]) * (z * jax.nn.sigmoid(z))
    sq = y * y
    all_ones = jnp.ones((LANES, LANES), BF16)
    tiles = ssd_group // LANES
    parts = []
    for gi in range(y.shape[1] // ssd_group):
        acc = sq[:, gi * ssd_group:gi * ssd_group + LANES]
        for ti in range(1, tiles):
            acc = acc + sq[:, gi * ssd_group + ti * LANES:gi * ssd_group + (ti + 1) * LANES]
        ms = _ones_mm(acc, all_ones) * (1.0 / ssd_group)
        parts.extend([ms] * tiles)
    ms = jnp.concatenate(parts, axis=1)
    ysd_ref[...] = (y * lax.rsqrt(ms + RMS_EPS) * nw_ref[...]).astype(ysd_ref.dtype)


def mixer_finish(o, bonus, gate, ln_w, ln_b, ys, xbc, proj, d_skip, norm_w, *, nb, cl, t, z_col0,
                 ssd_group, cg=512, interpret=False):
    tb = TOK_BLK
    cw = o.shape[-1]
    assert cl % tb == 0 and t % tb == 0 and z_col0 % cg == 0 and cg % ssd_group == 0
    off = cl // tb
    bps = t // tb
    zrow0 = nb * cl // tb
    zc = z_col0 // cg
    two = pl.BlockSpec((2, 1, tb, cg), lambda b, k, g: (0, b, k, g))
    shifted = pl.BlockSpec((1, tb, cg), lambda b, k, g: (b, k + off, g))
    per_c = pl.BlockSpec((1, cg), lambda b, k, g: (0, g))
    out = pl.BlockSpec((tb, cg), lambda b, k, g: (b * bps + k, g))
    return pl.pallas_call(
        functools.partial(_mixer_finish_kernel, ssd_group=ssd_group),
        grid=(nb, bps, cw // cg),
        in_specs=[two, shifted, shifted, per_c, per_c, two, shifted,
                  pl.BlockSpec((tb, cg), lambda b, k, g: (zrow0 + b * bps + k, zc + g)),
                  per_c, per_c],
        out_specs=[out, out],
        out_shape=[jax.ShapeDtypeStruct((nb * t, cw), BF16)] * 2,
        compiler_params=pltpu.CompilerParams(
            dimension_semantics=("arbitrary", "arbitrary", "arbitrary"),
            vmem_limit_bytes=VMEM_LIMIT),
        name="mixer_finish",
        interpret=interpret,
    )(o, bonus, gate, ln_w.reshape(1, cw), ln_b.reshape(1, cw), ys, xbc, proj,
      d_skip.reshape(1, cw), norm_w.reshape(1, cw))


def _col_form(zz, ng):
    b, tt, _, nh = zz.shape
    return jnp.transpose(zz.reshape(b, tt, 2, ng, nh // ng), (2, 0, 3, 1, 4))


def _row_form(zz, ng):
    b, tt, _, nh = zz.shape
    return jnp.transpose(zz.reshape(b, tt, 2, ng, nh // ng), (2, 0, 3, 4, 1))


COL_BLK = 512
LR_BLK = 768


def _by_group(z, ncg):
    rows, _, c = z.shape
    return jnp.transpose(z.reshape(rows, 2, ncg, c // ncg), (0, 2, 1, 3)).reshape(rows, 2 * c)


def _in_proj_layout(w_in, rwkv_mu, w2, a2, g2, w0, a0, cw, sw, xbc_w, nh):
    d = w_in.shape[0]
    rc = rwkv_mu.shape[0]
    lr = rc - 3 * cw
    dr, ir, gr = w2.shape[1], a2.shape[1], g2.shape[0]
    assert lr == 2 * dr + 2 * ir + gr and lr <= LR_BLK and (3 * cw) % LR_BLK == 0
    lr_off = 3 * cw
    z_off = -(-(lr_off + LR_BLK) // COL_BLK) * COL_BLK
    xbc_off = z_off + sw
    dt_off = xbc_off + xbc_w
    total = -(-(dt_off + 2 * nh) // COL_BLK) * COL_BLK
    zc = lambda n: jnp.zeros((d, n), w_in.dtype)
    w_p = jnp.concatenate([
        w_in[:, :rc], zc(z_off - rc), w_in[:, rc:rc + sw + xbc_w + 2 * nh],
        zc(total - dt_off - 2 * nh)], axis=1).astype(BF16)
    mu_w = -(-(lr_off + LR_BLK) // (3 * COL_BLK)) * 3 * COL_BLK
    mu_p = jnp.pad(rwkv_mu, (0, mu_w - rc)).reshape(1, mu_w)
    ncg = cw // COL_BLK
    wd = jnp.zeros((LR_BLK, 2, cw), F32).at[0:dr, 0].set(w2[0]).at[dr:2 * dr, 1].set(w2[1])
    ad = (jnp.zeros((LR_BLK, 2, cw), F32).at[2 * dr:2 * dr + ir, 0].set(a2[0])
          .at[2 * dr + ir:2 * dr + 2 * ir, 1].set(a2[1]))
    gd = jnp.zeros((LR_BLK, cw), F32).at[2 * dr + 2 * ir:lr].set(g2)
    return dict(w=w_p, mu=mu_p, lr_off=lr_off, z_off=z_off, xbc_off=xbc_off, dt_off=dt_off,
                w2g=_by_group(wd, ncg).astype(BF16), a2g=_by_group(ad, ncg).astype(BF16),
                g2g=gd.astype(BF16), w0g=_by_group(w0[None], ncg), a0g=_by_group(a0[None], ncg))


def kernel(x, c, ctx, c_ctx, w_mod, b_mod, norm1_g, norm2_g, w_in, w_out, rwkv_mu, rwkv_w0, rwkv_w2, rwkv_a0, rwkv_a2, rwkv_g2, rwkv_k_k, rwkv_k_a, rwkv_r_k, rwkv_ln_w, rwkv_ln_b, ssm_conv_w, ssm_conv_b, ssm_dt_bias, ssm_a_log, ssm_d, ssm_norm_w, peer_wq, peer_k1, peer_k2, peer_u, peer_v, final_g):
    assert w_in.shape[0] == 1, "single-layer block"
    nb, t, d = x.shape
    cl = ctx.shape[1]
    mod_rows = 8
    s = jnp.concatenate([c, c_ctx[None], jnp.zeros((mod_rows - nb - 1, d), F32)], axis=0)
    mod3 = mod_matmul(s, w_mod[0], b_mod[0]).reshape(mod_rows * N_MOD, 1, d)

    tm = 512
    nctx_blk = nb * cl // tm
    x_blk = t // tm
    h_all = jnp.concatenate([ctx.reshape(nb * cl, d), x.reshape(nb * t, d)], axis=0)
    cw = rwkv_k_k.shape[1]
    sw = ssm_norm_w.shape[1]
    xbc_w = ssm_conv_w.shape[2]
    nh = ssm_a_log.shape[2]
    ng = (xbc_w - sw) // (2 * SSM_STATE)
    lay = _in_proj_layout(w_in[0], rwkv_mu[0], rwkv_w2[0], rwkv_a2[0], rwkv_g2[0], rwkv_w0[0],
                          rwkv_a0[0], cw, sw, xbc_w, nh)
    proj = normmod_matmul(
        h_all, norm1_g[0], mod3, lay["w"],
        row_of_block=lambda i: jnp.where(i < nctx_blk, nb, (i - nctx_blk) // x_blk),
        shift_idx=0, scale_idx=1, n_mod=N_MOD, tm=tm, tn=COL_BLK)

    r, v, kk, ld, kd, bb, bonus, gate = rwkv_prep(
        proj, lay["mu"], lay["w2g"], lay["a2g"], lay["g2g"], lay["w0g"], lay["a0g"],
        rwkv_k_k[0], rwkv_k_a[0], rwkv_r_k[0].reshape(cw), nb=nb, cl=cl, t=t, cw=cw,
        lr_off=lay["lr_off"], lr_w=LR_BLK, grid_w=GRID_W, cg=COL_BLK)
    o = rwkv_scan(r, v, kk, ld, kd, bb, ctx_len=cl)

    xbc = ssd_prep(proj, ssm_conv_w[0], ssm_conv_b[0], nb=nb, cl=cl, t=t, col0=lay["xbc_off"],
                   cg=COL_BLK)
    dt_raw = proj[:, lay["dt_off"]:lay["dt_off"] + 2 * nh]
    dt_raw = jnp.concatenate([dt_raw[:nb * cl].reshape(nb, cl, 2, nh),
                              dt_raw[nb * cl:].reshape(nb, t, 2, nh)], axis=1)
    dt = jax.nn.softplus(dt_raw + ssm_dt_bias[0])
    la = dt * -jnp.exp(ssm_a_log[0])
    ys = ssd_scan(xbc, _col_form(dt, ng), _col_form(la, ng), _row_form(dt, ng), _row_form(la, ng),
                  ctx_len=cl, nst=SSM_STATE)

    y_rwkv, y_ssd = mixer_finish(
        o, bonus, gate, rwkv_ln_w[0], rwkv_ln_b[0], ys, xbc, proj,
        jnp.repeat(ssm_d[0], sw // nh), ssm_norm_w[0], nb=nb, cl=cl, t=t, z_col0=lay["z_off"],
        ssd_group=sw // ng, cg=COL_BLK)
    by_batch = lambda i: i // x_blk
    h1 = matmul_gated_resid(y_rwkv, y_ssd, w_out[0].astype(BF16), x.reshape(nb * t, d), mod3,
                            row_of_block=by_batch, gate_idx=2, n_mod=N_MOD, tm=tm, tn=1024)

    q, xn2 = normmod_matmul(h1, norm2_g[0], mod3, peer_wq[0].astype(BF16), row_of_block=by_batch,
                            shift_idx=3, scale_idx=4, n_mod=N_MOD, tm=tm, tn=512, emit_xn=True)
    s1, s2, e1, e2, tau = peer_scores(q, peer_k1[0], peer_k2[0], topk=PEER_TOPK)
    wt = peer_weights(peer_u[0].astype(BF16), xn2, s1, s2, e1, e2, tau)
    p = peer_out(wt, peer_v[0].astype(BF16))
    tf = 256
    out = final_norm(h1, p, mod3, final_g, row_of_block=lambda i: i // (t // tf), gate_idx=5,
                     n_mod=N_MOD, tm=tf)
    return out.reshape(nb, t, d)
```

```python
import functools
import math

import jax
import jax.numpy as jnp
from jax import lax
from jax.experimental import pallas as pl
from jax.experimental.pallas import tpu as pltpu

F32 = jnp.float32
BF16 = jnp.bfloat16
HI = lax.Precision.HIGHEST

RWKV_CHUNK = 64
RWKV_HEAD = 64
PAIR = 2 * RWKV_HEAD


def _mm(a, b, precision=None):
    return lax.dot_general(a, b, (((1,), (0,)), ((), ())), precision=precision,
                           preferred_element_type=F32)


def _mm_nt(a, b, precision=None):
    return lax.dot_general(a, b, (((1,), (1,)), ((), ())), precision=precision,
                           preferred_element_type=F32)


def _rwkv_masks(is_bwd):
    rho = lax.broadcasted_iota(jnp.int32, (PAIR, PAIR), 0)
    sig = lax.broadcasted_iota(jnp.int32, (PAIR, PAIR), 1)
    same64 = (rho // 64) == (sig // 64)
    same32 = (rho // 32) == (sig // 32)
    same16 = (rho // 16) == (sig // 16)
    sgn = jnp.where(is_bwd, -1, 1).astype(jnp.int32)
    before = ((rho - sig) * sgn > 0) & same64
    eye = rho == sig
    t = lax.broadcasted_iota(jnp.int32, (RWKV_CHUNK, RWKV_CHUNK), 0)
    s = lax.broadcasted_iota(jnp.int32, (RWKV_CHUNK, RWKV_CHUNK), 1)
    tri = jnp.where((t - s) * sgn >= 0, 1.0, 0.0).astype(F32)
    return dict(same64=same64, d16=same16, off32=same32 & ~same16, off64=same64 & ~same32,
                before=before, before_eq=before | eye, eye=eye, tri=tri)


def _each(f, *lists):
    return [f(*xs) for xs in zip(*lists)]


def _rwkv_chunk(r, v, kk, ld, kd, bb, S, m):
    zero = jnp.zeros((), F32)
    bf = lambda z: z.astype(BF16)
    bd, before, eye = m["same64"], m["before"], m["eye"]
    before_eq2 = jnp.concatenate([m["before_eq"]] * 2, axis=1)
    stack2 = lambda x: jnp.where(bd, jnp.concatenate([x, x], axis=0), zero)

    cum = _each(lambda x: _mm(m["tri"], x, HI), ld)
    tot = _each(lambda x: jnp.sum(x, axis=0, keepdims=True), ld)
    e_inv = _each(lambda c: jnp.exp(-c), cum)
    e_end = _each(lambda c, t: jnp.exp(t - c), cum, tot)
    a2 = _each(lambda k, c, x: stack2(-k * jnp.exp(c - x)), kk, cum, ld)
    r2 = _each(lambda x, c: stack2(x * jnp.exp(c)), r, cum)
    b2 = _each(lambda x, e: stack2(x * e), bb, e_inv)
    k2 = _each(lambda x, e: stack2(x * e), kd, e_inv)
    bkg = _each(lambda x, y, e: bf(jnp.concatenate([stack2(x * e), stack2(y * e)], axis=0)),
                bb, kd, e_end)
    v2 = _each(stack2, v)
    sc = _each(lambda a, rr, b, k: _mm_nt(bf(jnp.concatenate([a, rr], axis=0)),
                                          bf(jnp.concatenate([b, k], axis=0))), a2, r2, b2, k2)
    mm = _each(lambda s: jnp.where(before, s[:PAIR, :PAIR], zero), sc)
    nn = _each(lambda s: bf(jnp.where(before, s[:PAIR, PAIR:], zero)), sc)
    pq = _each(lambda s: bf(jnp.where(before_eq2, s[PAIR:], zero)), sc)

    md = _each(lambda x: bf(jnp.where(m["d16"], x, zero)), mm)
    m2 = _each(lambda x: bf(_mm(x, x)), md)
    t_inv = _each(lambda x: jnp.where(eye, 1.0, zero) + x, md)
    tp = _each(lambda t, p: _mm(jnp.concatenate([bf(t), p], axis=0), p), t_inv, m2)
    t_inv = _each(lambda t, x: t + x[:PAIR], t_inv, tp)
    m4 = _each(lambda x: bf(x[PAIR:]), tp)
    tp = _each(lambda t, p: _mm(jnp.concatenate([bf(t), p], axis=0), p), t_inv, m4)
    t_inv = _each(lambda t, x: t + x[:PAIR], t_inv, tp)
    m8 = _each(lambda x: bf(x[PAIR:]), tp)
    t_inv = _each(lambda t, p: t + _mm(bf(t), p), t_inv, m8)
    for off in ("off32", "off64"):
        mo = _each(lambda x: bf(jnp.where(m[off], x, zero)), mm)
        tb = _each(bf, t_inv)
        mt = _each(lambda a, b: bf(_mm(a, b)), mo, tb)
        t_inv = _each(lambda t, a, b: t + _mm(a, b), t_inv, tb, mt)
    tb = _each(bf, t_inv)

    at = _each(lambda x: x.T, a2)
    vt = _each(lambda x: bf(x.T), v2)
    rt = _each(lambda x: x.T, r2)
    nvt = _each(_mm_nt, vt, nn)
    x1 = _each(lambda a, n, t: _mm_nt(bf(jnp.concatenate([a, n], axis=0)), t), at, nvt, tb)
    atp = _each(lambda x: bf(x[:PAIR]), x1)
    cv = _each(lambda x, y: jnp.concatenate([bf(x[PAIR:]), y], axis=1), x1, vt)
    g = _each(lambda a, b, t: _mm(a, b[:PAIR]) + jnp.where(eye, jnp.exp(t), zero), atp, bkg, tot)
    rtp = _each(lambda x, a, p: x + _mm_nt(a, p[:, :PAIR]), rt, atp, pq)
    c2 = _each(_mm, cv, bkg)
    c3 = _each(_mm_nt, cv, pq)
    so = _each(lambda s, a, b: _mm(bf(s), bf(jnp.concatenate([a, b], axis=1))), S, g, rtp)
    s_new = _each(lambda x, c: x[:, :PAIR] + c, so, c2)
    o2 = _each(lambda x, c: (x[:, PAIR:] + c).T, so, c3)
    return _each(lambda x: x[:RWKV_CHUNK] + x[RWKV_CHUNK:], o2), s_new


def _rwkv_kernel(r_ref, v_ref, kk_ref, ld_ref, kd_ref, bb_ref, o_ref, s_ref, *, npair, nch):
    is_bwd = pl.program_id(0) == 1
    blk = pl.program_id(3)

    @pl.when(blk == 0)
    def _():
        s_ref[...] = jnp.zeros_like(s_ref)

    m = _rwkv_masks(is_bwd)
    lanes = [slice(p * PAIR, (p + 1) * PAIR) for p in range(npair)]

    def chunk(c, carry):
        ci = jnp.where(is_bwd, nch - 1 - c, c)
        rows = pl.ds(pl.multiple_of(ci * RWKV_CHUNK, RWKV_CHUNK), RWKV_CHUNK)
        o, s_new = _rwkv_chunk(
            [r_ref[0, rows, ln] for ln in lanes], [v_ref[0, rows, ln] for ln in lanes],
            [kk_ref[0, rows, ln] for ln in lanes], [ld_ref[0, 0, rows, ln] for ln in lanes],
            [kd_ref[0, 0, rows, ln] for ln in lanes], [bb_ref[0, 0, rows, ln] for ln in lanes],
            [s_ref[p] for p in range(npair)], m)
        for p in range(npair):
            s_ref[p] = s_new[p]
            o_ref[0, 0, rows, lanes[p]] = o[p]
        return carry

    lax.fori_loop(0, nch, chunk, 0)


def rwkv_scan(r, v, kk, ld, kd, bb, *, ctx_len, tb=256, lanes=1024, interpret=False):
    b, tt, c = r.shape
    assert ctx_len % tb == 0 and tt % tb == 0 and c % lanes == 0 and lanes % PAIR == 0
    nctx = ctx_len // tb
    nblk = tt // tb

    def tmap(d, k):
        fwd = k
        bwd = jnp.where(k < nctx, nctx - 1 - k, nblk - 1 - (k - nctx))
        return jnp.where(d == 1, bwd, fwd)

    shared = pl.BlockSpec((1, tb, lanes), lambda d, bi, h, k: (bi, tmap(d, k), h))
    direc = pl.BlockSpec((1, 1, tb, lanes), lambda d, bi, h, k: (d, bi, tmap(d, k), h))
    out = pl.BlockSpec((1, 1, tb, lanes),
                       lambda d, bi, h, k: (d, bi, tmap(d, jnp.maximum(k, nctx)) - nctx, h))
    npair = lanes // PAIR
    return pl.pallas_call(
        functools.partial(_rwkv_kernel, npair=npair, nch=tb // RWKV_CHUNK),
        grid=(2, b, c // lanes, nblk),
        in_specs=[shared, shared, shared, direc, direc, direc],
        out_specs=out,
        out_shape=jax.ShapeDtypeStruct((2, b, tt - ctx_len, c), F32),
        scratch_shapes=[pltpu.VMEM((npair, PAIR, PAIR), F32)],
        compiler_params=pltpu.CompilerParams(
            dimension_semantics=("arbitrary", "arbitrary", "arbitrary", "arbitrary")),
        name="rwkv_scan",
        interpret=interpret,
    )(r, v, kk, ld, kd, bb)


def _time_block_map(nctx, nblk):
    def tmap(d, k):
        bwd = jnp.where(k < nctx, nctx - 1 - k, nblk - 1 - (k - nctx))
        return jnp.where(d == 1, bwd, k)
    return tmap


SSD_P = 64


def _ssd_kernel(xs_ref, bm_ref, cm_ref, dtc_ref, lac_ref, dtr_ref, lar_ref, y_ref, st_ref, *, chunk):
    is_bwd = pl.program_id(0) == 1
    blk = pl.program_id(3)

    @pl.when(blk == 0)
    def _():
        st_ref[...] = jnp.zeros_like(st_ref)

    sgn = jnp.where(is_bwd, -1, 1).astype(jnp.int32)
    l_i = lax.broadcasted_iota(jnp.int32, (chunk, chunk), 0)
    s_i = lax.broadcasted_iota(jnp.int32, (chunk, chunk), 1)
    incl = (l_i - s_i) * sgn >= 0
    tri = jnp.where(incl, 1.0, 0.0).astype(F32)
    lane = lax.broadcasted_iota(jnp.int32, (chunk, PAIR), 1)
    first = lane < SSD_P
    first_row = first[:1]

    ngb = dtc_ref.shape[2]
    nh = dtc_ref.shape[-1]
    nst = bm_ref.shape[-1] // ngb
    groups = list(range(ngb))
    pairs = [(gi, q) for gi in groups for q in range(nh // 2)]
    heads = [(gi, h) for gi in groups for h in range(nh)]
    pair_lanes = {(gi, q): slice((gi * nh // 2 + q) * PAIR, (gi * nh // 2 + q + 1) * PAIR)
                  for gi, q in pairs}

    bmat = [bm_ref[0, :, gi * nst:(gi + 1) * nst] for gi in groups]
    cmat_b = [cm_ref[0, :, gi * nst:(gi + 1) * nst].astype(BF16) for gi in groups]
    dtc = [dtc_ref[0, 0, gi] for gi in groups]
    lac = [lac_ref[0, 0, gi] for gi in groups]
    dtr = [dtr_ref[0, 0, gi] for gi in groups]
    lar = [lar_ref[0, 0, gi] for gi in groups]

    cum_col = _each(lambda z: _mm(tri, z, HI), lac)
    cum_row = _each(lambda z: _mm_nt(z, tri, HI), lar)
    cb = _each(lambda c, b: _mm_nt(c, b.astype(BF16)), cmat_b, bmat)
    b_t = _each(lambda b: b.T.astype(BF16), bmat)
    tot = _each(lambda z: jnp.sum(z, axis=0, keepdims=True), lac)
    coef = _each(lambda t, c, z: jnp.exp(t - c) * z, tot, cum_col, dtc)
    ecum = _each(jnp.exp, cum_col)
    etot = _each(jnp.exp, tot)

    xp = [xs_ref[0, :, pair_lanes[p]] for p in pairs]
    xp_b = _each(lambda z: z.astype(BF16), xp)
    st = [st_ref[:, pair_lanes[p]] for p in pairs]
    wmat = [(cb[gi] * jnp.exp(jnp.where(incl, cum_col[gi][:, h:h + 1] - cum_row[gi][h:h + 1, :],
                                        -jnp.inf)) * dtr[gi][h:h + 1, :]).astype(BF16)
            for gi, h in heads]
    yh = [_mm(w, xp_b[i // 2]) for i, w in enumerate(wmat)]
    yoff = [_mm(cmat_b[gi], s.astype(BF16)) for (gi, q), s in zip(pairs, st)]
    snew = [_mm(b_t[gi], (x * jnp.where(first, coef[gi][:, 2 * q:2 * q + 1],
                                        coef[gi][:, 2 * q + 1:2 * q + 2])).astype(BF16))
            for (gi, q), x in zip(pairs, xp)]
    for i, (gi, q) in enumerate(pairs):
        h0, h1 = 2 * q, 2 * q + 1
        y = jnp.where(first, yh[2 * i], yh[2 * i + 1])
        y = y + yoff[i] * jnp.where(first, ecum[gi][:, h0:h0 + 1], ecum[gi][:, h1:h1 + 1])
        st_ref[:, pair_lanes[(gi, q)]] = (
            st[i] * jnp.where(first_row, etot[gi][:, h0:h0 + 1], etot[gi][:, h1:h1 + 1]) + snew[i])
        y_ref[0, 0, :, pair_lanes[(gi, q)]] = y


def ssd_scan(xbc, dtc, lac, dtr, lar, *, ctx_len, nst, chunk=256, ngb=2, interpret=False):
    b, tt, _ = xbc.shape
    ng, nh = dtc.shape[2], dtc.shape[4]
    gw = ngb * nh * SSD_P
    width = ng * nh * SSD_P
    bw = ngb * nst
    assert xbc.shape[2] == width + 2 * ng * nst and width % bw == 0 and ng % ngb == 0
    assert ctx_len % chunk == 0 and tt % chunk == 0 and nh % 2 == 0
    nctx, nblk = ctx_len // chunk, tt // chunk
    tmap = _time_block_map(nctx, nblk)
    b0 = width // bw
    c0 = b0 + ng // ngb
    return pl.pallas_call(
        functools.partial(_ssd_kernel, chunk=chunk),
        grid=(2, b, ng // ngb, nblk),
        in_specs=[
            pl.BlockSpec((1, chunk, gw), lambda d, bi, g, k: (bi, tmap(d, k), g)),
            pl.BlockSpec((1, chunk, bw), lambda d, bi, g, k: (bi, tmap(d, k), b0 + g)),
            pl.BlockSpec((1, chunk, bw), lambda d, bi, g, k: (bi, tmap(d, k), c0 + g)),
            pl.BlockSpec((1, 1, ngb, chunk, nh), lambda d, bi, g, k: (d, bi, g, tmap(d, k), 0)),
            pl.BlockSpec((1, 1, ngb, chunk, nh), lambda d, bi, g, k: (d, bi, g, tmap(d, k), 0)),
            pl.BlockSpec((1, 1, ngb, nh, chunk), lambda d, bi, g, k: (d, bi, g, 0, tmap(d, k))),
            pl.BlockSpec((1, 1, ngb, nh, chunk), lambda d, bi, g, k: (d, bi, g, 0, tmap(d, k))),
        ],
        out_specs=pl.BlockSpec(
            (1, 1, chunk, gw),
            lambda d, bi, g, k: (d, bi, tmap(d, jnp.maximum(k, nctx)) - nctx, g)),
        out_shape=jax.ShapeDtypeStruct((2, b, tt - ctx_len, width), F32),
        scratch_shapes=[pltpu.VMEM((nst, gw), F32)],
        compiler_params=pltpu.CompilerParams(
            dimension_semantics=("arbitrary", "arbitrary", "arbitrary", "arbitrary")),
        name="ssd_scan",
        interpret=interpret,
    )(xbc, xbc, xbc, dtc, lac, dtr, lar)


RMS_EPS = 1e-6
VMEM_LIMIT = 56 * 1024 * 1024


def _mod_kernel(s_ref, w_ref, b_ref, o_ref):
    s = s_ref[...]
    s = s * jax.nn.sigmoid(s)
    o_ref[...] = _mm(s, w_ref[...], HI) + b_ref[...]


def mod_matmul(s, w, bias, *, tn=512, interpret=False):
    m, k = s.shape
    n = w.shape[1]
    return pl.pallas_call(
        _mod_kernel,
        grid=(n // tn,),
        in_specs=[pl.BlockSpec((m, k), lambda j: (0, 0)),
                  pl.BlockSpec((k, tn), lambda j: (0, j)),
                  pl.BlockSpec((1, tn), lambda j: (0, j))],
        out_specs=pl.BlockSpec((m, tn), lambda j: (0, j)),
        out_shape=jax.ShapeDtypeStruct((m, n), F32),
        compiler_params=pltpu.CompilerParams(dimension_semantics=("arbitrary",),
                                             vmem_limit_bytes=VMEM_LIMIT),
        name="mod_matmul",
        interpret=interpret,
    )(s, w, bias.reshape(1, n))


def _normmod_matmul_kernel(h_ref, g_ref, sh_ref, sc_ref, w_ref, o_ref, *rest, emit_xn):
    xn_ref = rest[-1]

    @pl.when(pl.program_id(1) == 0)
    def _():
        h = h_ref[...]
        hn = h * lax.rsqrt(jnp.mean(h * h, axis=-1, keepdims=True) + RMS_EPS)
        xn = (hn * g_ref[...]) * (1.0 + sc_ref[0]) + sh_ref[0]
        xn_ref[...] = xn.astype(BF16)
        if emit_xn:
            rest[0][...] = xn.astype(BF16)

    o_ref[...] = _mm(xn_ref[...], w_ref[...])


def normmod_matmul(h, g, mod3, w, *, row_of_block, shift_idx, scale_idx, n_mod, tm, tn,
                   emit_xn=False, interpret=False):
    rows, d = h.shape
    n = w.shape[1]
    assert rows % tm == 0 and n % tn == 0
    out_shape = [jax.ShapeDtypeStruct((rows, n), F32)]
    out_specs = [pl.BlockSpec((tm, tn), lambda i, j: (i, j))]
    if emit_xn:
        out_shape.append(jax.ShapeDtypeStruct((rows, d), BF16))
        out_specs.append(pl.BlockSpec((tm, d), lambda i, j: (i, 0)))
    res = pl.pallas_call(
        functools.partial(_normmod_matmul_kernel, emit_xn=emit_xn),
        grid=(rows // tm, n // tn),
        in_specs=[
            pl.BlockSpec((tm, d), lambda i, j: (i, 0)),
            pl.BlockSpec((1, d), lambda i, j: (0, 0)),
            pl.BlockSpec((1, 1, d), lambda i, j: (row_of_block(i) * n_mod + shift_idx, 0, 0)),
            pl.BlockSpec((1, 1, d), lambda i, j: (row_of_block(i) * n_mod + scale_idx, 0, 0)),
            pl.BlockSpec((d, tn), lambda i, j: (0, j)),
        ],
        out_specs=out_specs,
        out_shape=out_shape,
        scratch_shapes=[pltpu.VMEM((tm, d), BF16)],
        compiler_params=pltpu.CompilerParams(dimension_semantics=("arbitrary", "arbitrary"),
                                             vmem_limit_bytes=VMEM_LIMIT),
        name="normmod_matmul",
        interpret=interpret,
    )(h, g.reshape(1, d), mod3, mod3, w)
    return res if emit_xn else res[0]


def _matmul_gated_resid_kernel(ya_ref, yb_ref, w_ref, x_ref, gt_ref, o_ref):
    ka = ya_ref.shape[1]
    acc = _mm(ya_ref[...], w_ref[:ka, :]) + _mm(yb_ref[...], w_ref[ka:, :])
    o_ref[...] = x_ref[...] + gt_ref[0] * acc


def matmul_gated_resid(ya, yb, w, resid, mod3, *, row_of_block, gate_idx, n_mod, tm, tn,
                       interpret=False):
    rows, ka = ya.shape
    kb = yb.shape[1]
    k, n = w.shape
    assert rows % tm == 0 and n % tn == 0 and ka + kb == k
    return pl.pallas_call(
        _matmul_gated_resid_kernel,
        grid=(rows // tm, n // tn),
        in_specs=[
            pl.BlockSpec((tm, ka), lambda i, j: (i, 0)),
            pl.BlockSpec((tm, kb), lambda i, j: (i, 0)),
            pl.BlockSpec((k, tn), lambda i, j: (0, j)),
            pl.BlockSpec((tm, tn), lambda i, j: (i, j)),
            pl.BlockSpec((1, 1, tn), lambda i, j: (row_of_block(i) * n_mod + gate_idx, 0, j)),
        ],
        out_specs=pl.BlockSpec((tm, tn), lambda i, j: (i, j)),
        out_shape=jax.ShapeDtypeStruct((rows, n), F32),
        compiler_params=pltpu.CompilerParams(dimension_semantics=("arbitrary", "arbitrary"),
                                             vmem_limit_bytes=VMEM_LIMIT),
        name="matmul_gated_resid",
        interpret=interpret,
    )(ya, yb, w, resid, mod3)


def _top_values(s, k):
    n = s.shape[0]
    row = lax.broadcasted_iota(jnp.int32, s.shape, 0)
    rank = jnp.full(s.shape, float(k), F32)
    vals = []
    for a in range(k):
        mx = jnp.max(s, axis=0, keepdims=True)
        first = jnp.min(jnp.where(s == mx, row, n), axis=0, keepdims=True)
        hit = row == first
        s = jnp.where(hit, -jnp.inf, s)
        rank = jnp.where(hit, float(a), rank)
        vals.append(mx)
    return jnp.concatenate(vals, axis=0), rank


def _peer_score_kernel(q_ref, k1_ref, k2_ref, cnt_ref, rank_ref, e1_ref, e2_ref, *, topk):
    nheads, nkeys, qd = k1_ref.shape
    for h in range(nheads):
        q1 = q_ref[:, h * 2 * qd:(h * 2 + 1) * qd]
        q2 = q_ref[:, (h * 2 + 1) * qd:(h * 2 + 2) * qd]
        s1 = _mm_nt(k1_ref[h], q1, HI)
        s2 = _mm_nt(k2_ref[h], q2, HI)
        v1, rank1 = _top_values(s1, topk)
        v2, rank2 = _top_values(s2, topk)
        cand = jnp.concatenate([v1[a:a + 1] + v2[:topk // (a + 1)] for a in range(topk)], axis=0)
        best, _ = _top_values(cand, topk)
        tau = best[topk - 1:topk]
        z = jnp.sum(jnp.exp(best - best[:1]), axis=0, keepdims=True)
        cnt = jnp.zeros_like(s1)
        for a in range(topk):
            cnt_a = jnp.sum(jnp.where(v1[a:a + 1] + v2 >= tau, 1.0, 0.0), axis=0, keepdims=True)
            cnt = jnp.where(rank1 == float(a), cnt_a, cnt)
        cnt_ref[h] = cnt.astype(cnt_ref.dtype)
        rank_ref[h] = rank2.astype(rank_ref.dtype)
        e1_ref[h] = jnp.exp(s1 - v1[:1]).astype(e1_ref.dtype)
        e2_ref[h] = (jnp.exp(s2 - v2[:1]) / z).astype(e2_ref.dtype)


def peer_scores(q, k1, k2, *, topk, tb=256, gate_dtype=BF16, interpret=False):
    t = q.shape[0]
    nheads, nkeys, qd = k1.shape
    big = pl.BlockSpec((nheads, nkeys, tb), lambda i: (0, 0, i))
    shape_of = lambda dt: jax.ShapeDtypeStruct((nheads, nkeys, t), dt)
    return pl.pallas_call(
        functools.partial(_peer_score_kernel, topk=topk),
        grid=(t // tb,),
        in_specs=[pl.BlockSpec((tb, q.shape[1]), lambda i: (i, 0)),
                  pl.BlockSpec(k1.shape, lambda i: (0, 0, 0)),
                  pl.BlockSpec(k2.shape, lambda i: (0, 0, 0))],
        out_specs=[big, big, big, big],
        out_shape=[shape_of(F32), shape_of(gate_dtype), shape_of(F32), shape_of(gate_dtype)],
        compiler_params=pltpu.CompilerParams(dimension_semantics=("arbitrary",),
                                             vmem_limit_bytes=VMEM_LIMIT),
        name="peer_scores",
        interpret=interpret,
    )(q, k1, k2)


def _peer_weight_kernel(u_ref, x_ref, cnt_ref, rank_ref, e1_ref, e2_ref, w_ref):
    nheads, nkeys, tb = rank_ref.shape
    gdt = rank_ref.dtype
    sub = 8 * 4 // jnp.dtype(gdt).itemsize
    a = _mm_nt(u_ref[...], x_ref[...])
    act = 0.5 * a * (1.0 + lax.erf(a * (1.0 / math.sqrt(2.0))))

    def rows_of(ref, h, ii):
        row = jnp.broadcast_to(ref[h, ii:ii + 1, :], (sub, tb)).astype(gdt)
        return jnp.concatenate([row] * (nkeys // sub), axis=0)

    for ii in range(u_ref.shape[0] // nkeys):
        gate = jnp.zeros((nkeys, tb), gdt)
        for h in range(nheads):
            sel = rank_ref[h] < rows_of(cnt_ref, h, ii)
            gate = gate + jnp.where(sel, e2_ref[h], jnp.zeros((), gdt)) * rows_of(e1_ref, h, ii)
        rows = slice(ii * nkeys, (ii + 1) * nkeys)
        w_ref[rows, :] = (act[rows].astype(gdt) * gate).astype(w_ref.dtype)


def peer_weights(u, xn, cnt, rank, e1, e2, *, te=1024, tb=512, interpret=False):
    ne, d = u.shape
    t = xn.shape[0]
    nheads, nkeys, _ = cnt.shape
    ni = te // nkeys
    s1_spec = pl.BlockSpec((nheads, ni, tb), lambda i, j: (0, j, i))
    s2_spec = pl.BlockSpec((nheads, nkeys, tb), lambda i, j: (0, 0, i))
    return pl.pallas_call(
        _peer_weight_kernel,
        grid=(t // tb, ne // te),
        in_specs=[pl.BlockSpec((te, d), lambda i, j: (j, 0)),
                  pl.BlockSpec((tb, d), lambda i, j: (i, 0)),
                  s1_spec, s2_spec, s1_spec, s2_spec],
        out_specs=pl.BlockSpec((te, tb), lambda i, j: (j, i)),
        out_shape=jax.ShapeDtypeStruct((ne, t), BF16),
        compiler_params=pltpu.CompilerParams(dimension_semantics=("arbitrary", "arbitrary"),
                                             vmem_limit_bytes=VMEM_LIMIT),
        name="peer_weights",
        interpret=interpret,
    )(u, xn, cnt, rank, e1, e2)


def _peer_out_kernel(w_ref, v_ref, o_ref):
    @pl.when(pl.program_id(1) == 0)
    def _():
        o_ref[...] = jnp.zeros_like(o_ref)

    o_ref[...] += lax.dot_general(w_ref[...], v_ref[...], (((0,), (0,)), ((), ())),
                                  preferred_element_type=F32)


def peer_out(wt, v, *, tb=512, te=1024, interpret=False):
    ne, t = wt.shape
    d = v.shape[1]
    return pl.pallas_call(
        _peer_out_kernel,
        grid=(t // tb, ne // te),
        in_specs=[pl.BlockSpec((te, tb), lambda i, j: (j, i)),
                  pl.BlockSpec((te, d), lambda i, j: (j, 0))],
        out_specs=pl.BlockSpec((tb, d), lambda i, j: (i, 0)),
        out_shape=jax.ShapeDtypeStruct((t, d), F32),
        compiler_params=pltpu.CompilerParams(dimension_semantics=("arbitrary", "arbitrary"),
                                             vmem_limit_bytes=VMEM_LIMIT),
        name="peer_out",
        interpret=interpret,
    )(wt, v)


def _final_kernel(h_ref, p_ref, gt_ref, g_ref, o_ref):
    h = h_ref[...] + gt_ref[0] * p_ref[...]
    o_ref[...] = h * lax.rsqrt(jnp.mean(h * h, axis=-1, keepdims=True) + RMS_EPS) * g_ref[...]


def final_norm(h, p, mod3, g, *, row_of_block, gate_idx, n_mod, tm=256, interpret=False):
    rows, d = h.shape
    blk = pl.BlockSpec((tm, d), lambda i: (i, 0))
    return pl.pallas_call(
        _final_kernel,
        grid=(rows // tm,),
        in_specs=[blk, blk,
                  pl.BlockSpec((1, 1, d), lambda i: (row_of_block(i) * n_mod + gate_idx, 0, 0)),
                  pl.BlockSpec((1, d), lambda i: (0, 0))],
        out_specs=blk,
        out_shape=jax.ShapeDtypeStruct((rows, d), F32),
        compiler_params=pltpu.CompilerParams(dimension_semantics=("arbitrary",),
                                             vmem_limit_bytes=VMEM_LIMIT),
        name="final_norm",
        interpret=interpret,
    )(h, p, mod3, g.reshape(1, d))


N_MOD = 6
GRID_W = 64
RWKV_GN_EPS = 64e-5
SSM_STATE = 128
PEER_TOPK = 16


LANES = 128
TOK_BLK = 256


def _block_ones(group):
    i = lax.broadcasted_iota(jnp.int32, (LANES, LANES), 0) // group
    j = lax.broadcasted_iota(jnp.int32, (LANES, LANES), 1) // group
    return jnp.where(i == j, 1.0, 0.0).astype(BF16)


def _ones_mm(x, ones):
    hi = x.astype(BF16)
    lo = (x - hi.astype(F32)).astype(BF16)
    return _mm(hi, ones) + _mm(lo, ones)


def _segsum(x, ones):
    return jnp.concatenate([_ones_mm(x[:, i * LANES:(i + 1) * LANES], ones)
                            for i in range(x.shape[1] // LANES)], axis=1)


def _softplus(x):
    return jnp.maximum(x, 0.0) + jnp.log(1.0 + jnp.exp(-jnp.abs(x)))


def _shift_mix(cur_ref, prev_ref, next_ref, mu_ref, is_ctx, top, bottom, grid_w):
    cur = cur_ref[...]
    n, w = cur.shape
    row = lax.broadcasted_iota(jnp.int32, (n, w), 0)
    lane = lax.broadcasted_iota(jnp.int32, (n, w), 1)
    before = pltpu.roll(cur, 1, axis=0)
    after = pltpu.roll(cur, n - 1, axis=0)
    col = row % grid_w
    left = jnp.where(col == 0, 0.0, before)
    right = jnp.where(col == grid_w - 1, 0.0, after)
    up = jnp.concatenate([prev_ref[...] * (1.0 - top), cur[:n - grid_w]], axis=0)
    down = jnp.concatenate([cur[grid_w:], next_ref[...] * (1.0 - bottom)], axis=0)
    slot = lane % 4
    sh_grid = jnp.where(slot == 0, left, jnp.where(slot == 1, right, jnp.where(slot == 2, up, down)))
    sh_seq = jnp.where(lane % 2 == 0, jnp.where(row == 0, 0.0, before),
                       jnp.where(row == n - 1, 0.0, after))
    shifted = jnp.where(is_ctx, sh_seq, sh_grid)
    return cur + mu_ref[...] * (shifted - cur)


def _rwkv_prep_kernel(r_c, r_p, r_n, k_c, k_p, k_n, v_c, v_p, v_n, l_c, l_p, l_n,
                      mu_r, mu_k, mu_v, mu_l, w2_ref, a2_ref, g2_ref, w0_ref, a0_ref,
                      kk_ref, ka_ref, rk_ref,
                      r_o, v_o, kkn_o, ld_o, kd_o, bb_o, bonus_o, gate_o,
                      *, nctx_blk, blk_per_seq, grid_w):
    i = pl.program_id(0)
    is_ctx = i < nctx_blk
    tblk = (i - nctx_blk) % blk_per_seq
    top = jnp.where(tblk == 0, 1.0, 0.0)
    bottom = jnp.where(tblk == blk_per_seq - 1, 1.0, 0.0)
    mix = functools.partial(_shift_mix, is_ctx=is_ctx, top=top, bottom=bottom, grid_w=grid_w)
    r = mix(r_c, r_p, r_n, mu_r)
    k = mix(k_c, k_p, k_n, mu_k)
    v = mix(v_c, v_p, v_n, mu_v)
    low = mix(l_c, l_p, l_n, mu_l)
    cw = r.shape[1]

    w_low = _mm(jnp.tanh(low).astype(BF16), w2_ref[...])
    a_low = _mm(low.astype(BF16), a2_ref[...])
    gate_o[0] = _mm(jax.nn.sigmoid(low).astype(BF16), g2_ref[...])

    ones = _block_ones(RWKV_HEAD)
    kraw = k * kk_ref[...]
    kkn = kraw * lax.rsqrt(jnp.maximum(_segsum(kraw * kraw, ones), 1e-24))
    ksum = jnp.zeros_like(k)
    for d in range(2):
        cols = slice(d * cw, (d + 1) * cw)
        w = -_softplus(-(w0_ref[:, cols] + w_low[:, cols])) - 0.5
        a = jax.nn.sigmoid(a0_ref[:, cols] + a_low[:, cols])
        k_dir = k * (1.0 + (a - 1.0) * ka_ref[...])
        ld_o[d, 0] = -jnp.exp(w)
        kd_o[d, 0] = k_dir
        bb_o[d, 0] = kkn * a
        ksum = ksum + k_dir
    r_o[0] = r
    v_o[0] = v
    kkn_o[0] = kkn
    bonus_o[0] = _segsum(r * ksum * rk_ref[...], ones) * v


def rwkv_prep(proj, mu_p, w2g, a2g, g2g, w0g, a0g, k_k, k_a, r_k, *, nb, cl, t, cw, lr_off, lr_w,
              grid_w, cg=512, interpret=False):
    tb = TOK_BLK
    assert cl == tb and t % tb == 0 and tb % grid_w == 0 and cw % cg == 0 and lr_off % lr_w == 0
    nctx_blk = nb
    blk_per_seq = t // tb
    nblk = nctx_blk + nb * blk_per_seq
    hb = tb // grid_w
    last_h = nblk * hb - 1
    ncg = cw // cg

    def seg(col0, width):
        cb = col0 // width
        return [pl.BlockSpec((tb, width), lambda i, g, cb=cb, s=(width == cg): (i, cb + g * s)),
                pl.BlockSpec((grid_w, width),
                             lambda i, g, cb=cb, s=(width == cg): (jnp.maximum(i * hb - 1, 0), cb + g * s)),
                pl.BlockSpec((grid_w, width),
                             lambda i, g, cb=cb, s=(width == cg): (jnp.minimum((i + 1) * hb, last_h), cb + g * s))]

    def mu_spec(col0, width):
        cb = col0 // width
        return pl.BlockSpec((1, width), lambda i, g, cb=cb, s=(width == cg): (0, cb + g * s))

    def tok(i):
        b = jnp.where(i < nctx_blk, i, (i - nctx_blk) // blk_per_seq)
        tk = jnp.where(i < nctx_blk, 0, 1 + (i - nctx_blk) % blk_per_seq)
        return b, tk

    def out1(i, g):
        b, tk = tok(i)
        return (b, tk, g)

    def out2(i, g):
        b, tk = tok(i)
        return (0, b, tk, g)

    per_c = pl.BlockSpec((1, cg), lambda i, g: (0, g))
    one_shape = jax.ShapeDtypeStruct((nb, cl + t, cw), F32)
    two_shape = jax.ShapeDtypeStruct((2, nb, cl + t, cw), F32)
    one_spec = pl.BlockSpec((1, tb, cg), out1)
    two_spec = pl.BlockSpec((2, 1, tb, cg), out2)
    return pl.pallas_call(
        functools.partial(_rwkv_prep_kernel, nctx_blk=nctx_blk, blk_per_seq=blk_per_seq,
                          grid_w=grid_w),
        grid=(nblk, ncg),
        in_specs=(seg(0, cg) + seg(cw, cg) + seg(2 * cw, cg) + seg(lr_off, lr_w)
                  + [mu_spec(0, cg), mu_spec(cw, cg), mu_spec(2 * cw, cg), mu_spec(lr_off, lr_w)]
                  + [pl.BlockSpec((lr_w, 2 * cg), lambda i, g: (0, g)),
                     pl.BlockSpec((lr_w, 2 * cg), lambda i, g: (0, g)),
                     pl.BlockSpec((lr_w, cg), lambda i, g: (0, g)),
                     pl.BlockSpec((1, 2 * cg), lambda i, g: (0, g)),
                     pl.BlockSpec((1, 2 * cg), lambda i, g: (0, g)),
                     per_c, per_c, per_c]),
        out_specs=[one_spec, one_spec, one_spec, two_spec, two_spec, two_spec, one_spec, one_spec],
        out_shape=[one_shape] * 3 + [two_shape] * 3 + [one_shape] * 2,
        compiler_params=pltpu.CompilerParams(dimension_semantics=("arbitrary", "arbitrary"),
                                             vmem_limit_bytes=VMEM_LIMIT),
        name="rwkv_prep",
        interpret=interpret,
    )(*([proj] * 12), mu_p, mu_p, mu_p, mu_p, w2g, a2g, g2g, w0g, a0g,
      k_k.reshape(1, cw), k_a.reshape(1, cw), r_k.reshape(1, cw))


def _ssd_prep_kernel(c_ref, p_ref, n_ref, w_ref, b_ref, o_ref, *, nctx_blk, blk_per_seq):
    i = pl.program_id(0)
    is_ctx = i < nctx_blk
    tblk = (i - nctx_blk) % blk_per_seq
    first = jnp.logical_or(is_ctx, tblk == 0)
    last = jnp.logical_or(is_ctx, tblk == blk_per_seq - 1)
    cur = c_ref[...]
    n = cur.shape[0]
    row = lax.broadcasted_iota(jnp.int32, cur.shape, 0)
    halo = p_ref.shape[0]
    prev_row = p_ref[halo - 1:halo, :] * jnp.where(first, 0.0, 1.0)
    next_row = n_ref[0:1, :] * jnp.where(last, 0.0, 1.0)
    before = jnp.where(row == 0, prev_row, pltpu.roll(cur, 1, axis=0))
    after = jnp.where(row == n - 1, next_row, pltpu.roll(cur, n - 1, axis=0))
    y = w_ref[0:1, :] * before + w_ref[1:2, :] * cur + w_ref[2:3, :] * after + b_ref[...]
    o_ref[0] = y * jax.nn.sigmoid(y)


def ssd_prep(proj, conv_w, conv_b, *, nb, cl, t, col0, cg=512, interpret=False):
    tb = TOK_BLK
    halo = 8
    xbc_w = conv_w.shape[1]
    assert conv_w.shape[0] == 3 and cl == tb and t % tb == 0 and col0 % cg == 0 and xbc_w % cg == 0
    nctx_blk = nb
    blk_per_seq = t // tb
    nblk = nctx_blk + nb * blk_per_seq
    hb = tb // halo
    last_h = nblk * hb - 1
    cb = col0 // cg

    def out_map(i, g):
        b = jnp.where(i < nctx_blk, i, (i - nctx_blk) // blk_per_seq)
        tk = jnp.where(i < nctx_blk, 0, 1 + (i - nctx_blk) % blk_per_seq)
        return (b, tk, g)

    return pl.pallas_call(
        functools.partial(_ssd_prep_kernel, nctx_blk=nctx_blk, blk_per_seq=blk_per_seq),
        grid=(nblk, xbc_w // cg),
        in_specs=[pl.BlockSpec((tb, cg), lambda i, g: (i, cb + g)),
                  pl.BlockSpec((halo, cg), lambda i, g: (jnp.maximum(i * hb - 1, 0), cb + g)),
                  pl.BlockSpec((halo, cg), lambda i, g: (jnp.minimum((i + 1) * hb, last_h), cb + g)),
                  pl.BlockSpec((3, cg), lambda i, g: (0, g)),
                  pl.BlockSpec((1, cg), lambda i, g: (0, g))],
        out_specs=pl.BlockSpec((1, tb, cg), out_map),
        out_shape=jax.ShapeDtypeStruct((nb, cl + t, xbc_w), F32),
        compiler_params=pltpu.CompilerParams(dimension_semantics=("arbitrary", "arbitrary"),
                                             vmem_limit_bytes=VMEM_LIMIT),
        name="ssd_prep",
        interpret=interpret,
    )(proj, proj, proj, conv_w, conv_b.reshape(1, xbc_w))


def _mixer_finish_kernel(o_ref, bonus_ref, gate_ref, lnw_ref, lnb_ref,
                         ys_ref, xs_ref, z_ref, dsk_ref, nw_ref, yr_ref, ysd_ref, *, ssd_group):
    o = o_ref[0, 0] + o_ref[1, 0]
    ones = _block_ones(RWKV_HEAD)
    mean = _segsum(o, ones) * (1.0 / RWKV_HEAD)
    cen = o - mean
    var = _segsum(cen * cen, ones) * (1.0 / RWKV_HEAD)
    on = cen * lax.rsqrt(var + RWKV_GN_EPS)
    yr_ref[...] = ((on * lnw_ref[...] + lnb_ref[...] + bonus_ref[0]) * gate_ref[0]).astype(yr_ref.dtype)

    z = z_ref[...]
    y = (ys_ref[0, 0] + ys_ref[1, 0] + dsk_ref[...] * xs_ref[0]) * (z * jax.nn.sigmoid(z))
    sq = y * y
    all_ones = jnp.ones((LANES, LANES), BF16)
    tiles = ssd_group // LANES
    parts = []
    for gi in range(y.shape[1] // ssd_group):
        acc = sq[:, gi * ssd_group:gi * ssd_group + LANES]
        for ti in range(1, tiles):
            acc = acc + sq[:, gi * ssd_group + ti * LANES:gi * ssd_group + (ti + 1) * LANES]
        ms = _ones_mm(acc, all_ones) * (1.0 / ssd_group)
        parts.extend([ms] * tiles)
    ms = jnp.concatenate(parts, axis=1)
    ysd_ref[...] = (y * lax.rsqrt(ms + RMS_EPS) * nw_ref[...]).astype(ysd_ref.dtype)


def mixer_finish(o, bonus, gate, ln_w, ln_b, ys, xbc, proj, d_skip, norm_w, *, nb, cl, t, z_col0,
                 ssd_group, cg=512, interpret=False):
    tb = TOK_BLK
    cw = o.shape[-1]
    assert cl % tb == 0 and t % tb == 0 and z_col0 % cg == 0 and cg % ssd_group == 0
    off = cl // tb
    bps = t // tb
    zrow0 = nb * cl // tb
    zc = z_col0 // cg
    two = pl.BlockSpec((2, 1, tb, cg), lambda b, k, g: (0, b, k, g))
    shifted = pl.BlockSpec((1, tb, cg), lambda b, k, g: (b, k + off, g))
    per_c = pl.BlockSpec((1, cg), lambda b, k, g: (0, g))
    out = pl.BlockSpec((tb, cg), lambda b, k, g: (b * bps + k, g))
    return pl.pallas_call(
        functools.partial(_mixer_finish_kernel, ssd_group=ssd_group),
        grid=(nb, bps, cw // cg),
        in_specs=[two, shifted, shifted, per_c, per_c, two, shifted,
                  pl.BlockSpec((tb, cg), lambda b, k, g: (zrow0 + b * bps + k, zc + g)),
                  per_c, per_c],
        out_specs=[out, out],
        out_shape=[jax.ShapeDtypeStruct((nb * t, cw), BF16)] * 2,
        compiler_params=pltpu.CompilerParams(
            dimension_semantics=("arbitrary", "arbitrary", "arbitrary"),
            vmem_limit_bytes=VMEM_LIMIT),
        name="mixer_finish",
        interpret=interpret,
    )(o, bonus, gate, ln_w.reshape(1, cw), ln_b.reshape(1, cw), ys, xbc, proj,
      d_skip.reshape(1, cw), norm_w.reshape(1, cw))


def _col_form(zz, ng):
    b, tt, _, nh = zz.shape
    return jnp.transpose(zz.reshape(b, tt, 2, ng, nh // ng), (2, 0, 3, 1, 4))


def _row_form(zz, ng):
    b, tt, _, nh = zz.shape
    return jnp.transpose(zz.reshape(b, tt, 2, ng, nh // ng), (2, 0, 3, 4, 1))


COL_BLK = 512
LR_BLK = 768


def _by_group(z, ncg):
    rows, _, c = z.shape
    return jnp.transpose(z.reshape(rows, 2, ncg, c // ncg), (0, 2, 1, 3)).reshape(rows, 2 * c)


def _in_proj_layout(w_in, rwkv_mu, w2, a2, g2, w0, a0, cw, sw, xbc_w, nh):
    d = w_in.shape[0]
    rc = rwkv_mu.shape[0]
    lr = rc - 3 * cw
    dr, ir, gr = w2.shape[1], a2.shape[1], g2.shape[0]
    assert lr == 2 * dr + 2 * ir + gr and lr <= LR_BLK and (3 * cw) % LR_BLK == 0
    lr_off = 3 * cw
    z_off = -(-(lr_off + LR_BLK) // COL_BLK) * COL_BLK
    xbc_off = z_off + sw
    dt_off = xbc_off + xbc_w
    total = -(-(dt_off + 2 * nh) // COL_BLK) * COL_BLK
    zc = lambda n: jnp.zeros((d, n), w_in.dtype)
    w_p = jnp.concatenate([
        w_in[:, :rc], zc(z_off - rc), w_in[:, rc:rc + sw + xbc_w + 2 * nh],
        zc(total - dt_off - 2 * nh)], axis=1).astype(BF16)
    mu_w = -(-(lr_off + LR_BLK) // (3 * COL_BLK)) * 3 * COL_BLK
    mu_p = jnp.pad(rwkv_mu, (0, mu_w - rc)).reshape(1, mu_w)
    ncg = cw // COL_BLK
    wd = jnp.zeros((LR_BLK, 2, cw), F32).at[0:dr, 0].set(w2[0]).at[dr:2 * dr, 1].set(w2[1])
    ad = (jnp.zeros((LR_BLK, 2, cw), F32).at[2 * dr:2 * dr + ir, 0].set(a2[0])
          .at[2 * dr + ir:2 * dr + 2 * ir, 1].set(a2[1]))
    gd = jnp.zeros((LR_BLK, cw), F32).at[2 * dr + 2 * ir:lr].set(g2)
    return dict(w=w_p, mu=mu_p, lr_off=lr_off, z_off=z_off, xbc_off=xbc_off, dt_off=dt_off,
                w2g=_by_group(wd, ncg).astype(BF16), a2g=_by_group(ad, ncg).astype(BF16),
                g2g=gd.astype(BF16), w0g=_by_group(w0[None], ncg), a0g=_by_group(a0[None], ncg))


def kernel(x, c, ctx, c_ctx, w_mod, b_mod, norm1_g, norm2_g, w_in, w_out, rwkv_mu, rwkv_w0, rwkv_w2, rwkv_a0, rwkv_a2, rwkv_g2, rwkv_k_k, rwkv_k_a, rwkv_r_k, rwkv_ln_w, rwkv_ln_b, ssm_conv_w, ssm_conv_b, ssm_dt_bias, ssm_a_log, ssm_d, ssm_norm_w, peer_wq, peer_k1, peer_k2, peer_u, peer_v, final_g):
    assert w_in.shape[0] == 1, "single-layer block"
    nb, t, d = x.shape
    cl = ctx.shape[1]
    mod_rows = 8
    s = jnp.concatenate([c, c_ctx[None], jnp.zeros((mod_rows - nb - 1, d), F32)], axis=0)
    mod3 = mod_matmul(s, w_mod[0], b_mod[0]).reshape(mod_rows * N_MOD, 1, d)

    tm = 512
    nctx_blk = nb * cl // tm
    x_blk = t // tm
    h_all = jnp.concatenate([ctx.reshape(nb * cl, d), x.reshape(nb * t, d)], axis=0)
    cw = rwkv_k_k.shape[1]
    sw = ssm_norm_w.shape[1]
    xbc_w = ssm_conv_w.shape[2]
    nh = ssm_a_log.shape[2]
    ng = (xbc_w - sw) // (2 * SSM_STATE)
    lay = _in_proj_layout(w_in[0], rwkv_mu[0], rwkv_w2[0], rwkv_a2[0], rwkv_g2[0], rwkv_w0[0],
                          rwkv_a0[0], cw, sw, xbc_w, nh)
    proj = normmod_matmul(
        h_all, norm1_g[0], mod3, lay["w"],
        row_of_block=lambda i: jnp.where(i < nctx_blk, nb, (i - nctx_blk) // x_blk),
        shift_idx=0, scale_idx=1, n_mod=N_MOD, tm=tm, tn=COL_BLK)

    r, v, kk, ld, kd, bb, bonus, gate = rwkv_prep(
        proj, lay["mu"], lay["w2g"], lay["a2g"], lay["g2g"], lay["w0g"], lay["a0g"],
        rwkv_k_k[0], rwkv_k_a[0], rwkv_r_k[0].reshape(cw), nb=nb, cl=cl, t=t, cw=cw,
        lr_off=lay["lr_off"], lr_w=LR_BLK, grid_w=GRID_W, cg=COL_BLK)
    o = rwkv_scan(r, v, kk, ld, kd, bb, ctx_len=cl)

    xbc = ssd_prep(proj, ssm_conv_w[0], ssm_conv_b[0], nb=nb, cl=cl, t=t, col0=lay["xbc_off"],
                   cg=2 * COL_BLK)
    dt_raw = proj[:, lay["dt_off"]:lay["dt_off"] + 2 * nh]
    dt_raw = jnp.concatenate([dt_raw[:nb * cl].reshape(nb, cl, 2, nh),
                              dt_raw[nb * cl:].reshape(nb, t, 2, nh)], axis=1)
    dt = jax.nn.softplus(dt_raw + ssm_dt_bias[0])
    la = dt * -jnp.exp(ssm_a_log[0])
    ys = ssd_scan(xbc, _col_form(dt, ng), _col_form(la, ng), _row_form(dt, ng), _row_form(la, ng),
                  ctx_len=cl, nst=SSM_STATE)

    y_rwkv, y_ssd = mixer_finish(
        o, bonus, gate, rwkv_ln_w[0], rwkv_ln_b[0], ys, xbc, proj,
        jnp.repeat(ssm_d[0], sw // nh), ssm_norm_w[0], nb=nb, cl=cl, t=t, z_col0=lay["z_off"],
        ssd_group=sw // ng, cg=COL_BLK)
    by_batch = lambda i: i // x_blk
    h1 = matmul_gated_resid(y_rwkv, y_ssd, w_out[0].astype(BF16), x.reshape(nb * t, d), mod3,
                            row_of_block=by_batch, gate_idx=2, n_mod=N_MOD, tm=tm, tn=1024)

    q, xn2 = normmod_matmul(h1, norm2_g[0], mod3, peer_wq[0].astype(BF16), row_of_block=by_batch,
                            shift_idx=3, scale_idx=4, n_mod=N_MOD, tm=tm, tn=512, emit_xn=True)
    cnt, rank, e1, e2 = peer_scores(q, peer_k1[0], peer_k2[0], topk=PEER_TOPK)
    wt = peer_weights(peer_u[0].astype(BF16), xn2, cnt, rank, e1, e2)
    p = peer_out(wt, peer_v[0].astype(BF16))
    tf = 256
    out = final_norm(h1, p, mod3, final_g, row_of_block=lambda i: i // (t // tf), gate_idx=5,
                     n_mod=N_MOD, tm=tf)
    return out.reshape(nb, t, d)
```

```python
import functools
import math

import jax
import jax.numpy as jnp
from jax import lax
from jax.experimental import pallas as pl
from jax.experimental.pallas import tpu as pltpu

F32 = jnp.float32
BF16 = jnp.bfloat16
HI = lax.Precision.HIGHEST

RWKV_CHUNK = 64
RWKV_HEAD = 64
PAIR = 2 * RWKV_HEAD


def _mm(a, b, precision=None):
    return lax.dot_general(a, b, (((1,), (0,)), ((), ())), precision=precision,
                           preferred_element_type=F32)


def _mm_nt(a, b, precision=None):
    return lax.dot_general(a, b, (((1,), (1,)), ((), ())), precision=precision,
                           preferred_element_type=F32)


def _rwkv_masks(is_bwd):
    rho = lax.broadcasted_iota(jnp.int32, (PAIR, PAIR), 0)
    sig = lax.broadcasted_iota(jnp.int32, (PAIR, PAIR), 1)
    same64 = (rho // 64) == (sig // 64)
    same32 = (rho // 32) == (sig // 32)
    same16 = (rho // 16) == (sig // 16)
    sgn = jnp.where(is_bwd, -1, 1).astype(jnp.int32)
    before = ((rho - sig) * sgn > 0) & same64
    eye = rho == sig
    t = lax.broadcasted_iota(jnp.int32, (RWKV_CHUNK, RWKV_CHUNK), 0)
    s = lax.broadcasted_iota(jnp.int32, (RWKV_CHUNK, RWKV_CHUNK), 1)
    tri = jnp.where((t - s) * sgn >= 0, 1.0, 0.0).astype(F32)
    return dict(same64=same64, d16=same16, off32=same32 & ~same16, off64=same64 & ~same32,
                before=before, before_eq=before | eye, eye=eye, tri=tri)


def _each(f, *lists):
    return [f(*xs) for xs in zip(*lists)]


def _rwkv_chunk(r, v, kk, ld, kd, bb, S, m):
    zero = jnp.zeros((), F32)
    bf = lambda z: z.astype(BF16)
    bd, before, eye = m["same64"], m["before"], m["eye"]
    before_eq2 = jnp.concatenate([m["before_eq"]] * 2, axis=1)
    stack2 = lambda x: jnp.where(bd, jnp.concatenate([x, x], axis=0), zero)

    cum = _each(lambda x: _mm(m["tri"], x, HI), ld)
    tot = _each(lambda x: jnp.sum(x, axis=0, keepdims=True), ld)
    e_inv = _each(lambda c: jnp.exp(-c), cum)
    e_end = _each(lambda c, t: jnp.exp(t - c), cum, tot)
    a2 = _each(lambda k, c, x: stack2(-k * jnp.exp(c - x)), kk, cum, ld)
    r2 = _each(lambda x, c: stack2(x * jnp.exp(c)), r, cum)
    b2 = _each(lambda x, e: stack2(x * e), bb, e_inv)
    k2 = _each(lambda x, e: stack2(x * e), kd, e_inv)
    bkg = _each(lambda x, y, e: bf(jnp.concatenate([stack2(x * e), stack2(y * e)], axis=0)),
                bb, kd, e_end)
    v2 = _each(stack2, v)
    sc = _each(lambda a, rr, b, k: _mm_nt(bf(jnp.concatenate([a, rr], axis=0)),
                                          bf(jnp.concatenate([b, k], axis=0))), a2, r2, b2, k2)
    mm = _each(lambda s: jnp.where(before, s[:PAIR, :PAIR], zero), sc)
    nn = _each(lambda s: bf(jnp.where(before, s[:PAIR, PAIR:], zero)), sc)
    pq = _each(lambda s: bf(jnp.where(before_eq2, s[PAIR:], zero)), sc)

    md = _each(lambda x: bf(jnp.where(m["d16"], x, zero)), mm)
    m2 = _each(lambda x: bf(_mm(x, x)), md)
    t_inv = _each(lambda x: jnp.where(eye, 1.0, zero) + x, md)
    tp = _each(lambda t, p: _mm(jnp.concatenate([bf(t), p], axis=0), p), t_inv, m2)
    t_inv = _each(lambda t, x: t + x[:PAIR], t_inv, tp)
    m4 = _each(lambda x: bf(x[PAIR:]), tp)
    tp = _each(lambda t, p: _mm(jnp.concatenate([bf(t), p], axis=0), p), t_inv, m4)
    t_inv = _each(lambda t, x: t + x[:PAIR], t_inv, tp)
    m8 = _each(lambda x: bf(x[PAIR:]), tp)
    t_inv = _each(lambda t, p: t + _mm(bf(t), p), t_inv, m8)
    for off in ("off32", "off64"):
        mo = _each(lambda x: bf(jnp.where(m[off], x, zero)), mm)
        tb = _each(bf, t_inv)
        mt = _each(lambda a, b: bf(_mm(a, b)), mo, tb)
        t_inv = _each(lambda t, a, b: t + _mm(a, b), t_inv, tb, mt)
    tb = _each(bf, t_inv)

    at = _each(lambda x: x.T, a2)
    vt = _each(lambda x: bf(x.T), v2)
    rt = _each(lambda x: x.T, r2)
    nvt = _each(_mm_nt, vt, nn)
    x1 = _each(lambda a, n, t: _mm_nt(bf(jnp.concatenate([a, n], axis=0)), t), at, nvt, tb)
    atp = _each(lambda x: bf(x[:PAIR]), x1)
    cv = _each(lambda x, y: jnp.concatenate([bf(x[PAIR:]), y], axis=1), x1, vt)
    g = _each(lambda a, b, t: _mm(a, b[:PAIR]) + jnp.where(eye, jnp.exp(t), zero), atp, bkg, tot)
    rtp = _each(lambda x, a, p: x + _mm_nt(a, p[:, :PAIR]), rt, atp, pq)
    c2 = _each(_mm, cv, bkg)
    c3 = _each(_mm_nt, cv, pq)
    so = _each(lambda s, a, b: _mm(bf(s), bf(jnp.concatenate([a, b], axis=1))), S, g, rtp)
    s_new = _each(lambda x, c: x[:, :PAIR] + c, so, c2)
    o2 = _each(lambda x, c: (x[:, PAIR:] + c).T, so, c3)
    return _each(lambda x: x[:RWKV_CHUNK] + x[RWKV_CHUNK:], o2), s_new


def _rwkv_kernel(r_ref, v_ref, kk_ref, ld_ref, kd_ref, bb_ref, o_ref, s_ref, *, npair, nch):
    is_bwd = pl.program_id(0) == 1
    blk = pl.program_id(3)

    @pl.when(blk == 0)
    def _():
        s_ref[...] = jnp.zeros_like(s_ref)

    m = _rwkv_masks(is_bwd)
    lanes = [slice(p * PAIR, (p + 1) * PAIR) for p in range(npair)]

    def chunk(c, carry):
        ci = jnp.where(is_bwd, nch - 1 - c, c)
        rows = pl.ds(pl.multiple_of(ci * RWKV_CHUNK, RWKV_CHUNK), RWKV_CHUNK)
        o, s_new = _rwkv_chunk(
            [r_ref[0, rows, ln] for ln in lanes], [v_ref[0, rows, ln] for ln in lanes],
            [kk_ref[0, rows, ln] for ln in lanes], [ld_ref[0, 0, rows, ln] for ln in lanes],
            [kd_ref[0, 0, rows, ln] for ln in lanes], [bb_ref[0, 0, rows, ln] for ln in lanes],
            [s_ref[p] for p in range(npair)], m)
        for p in range(npair):
            s_ref[p] = s_new[p]
            o_ref[0, 0, rows, lanes[p]] = o[p]
        return carry

    lax.fori_loop(0, nch, chunk, 0)


def rwkv_scan(r, v, kk, ld, kd, bb, *, ctx_len, tb=256, lanes=1024, interpret=False):
    b, tt, c = r.shape
    assert ctx_len % tb == 0 and tt % tb == 0 and c % lanes == 0 and lanes % PAIR == 0
    nctx = ctx_len // tb
    nblk = tt // tb

    def tmap(d, k):
        fwd = k
        bwd = jnp.where(k < nctx, nctx - 1 - k, nblk - 1 - (k - nctx))
        return jnp.where(d == 1, bwd, fwd)

    shared = pl.BlockSpec((1, tb, lanes), lambda d, bi, h, k: (bi, tmap(d, k), h))
    direc = pl.BlockSpec((1, 1, tb, lanes), lambda d, bi, h, k: (d, bi, tmap(d, k), h))
    out = pl.BlockSpec((1, 1, tb, lanes),
                       lambda d, bi, h, k: (d, bi, tmap(d, jnp.maximum(k, nctx)) - nctx, h))
    npair = lanes // PAIR
    return pl.pallas_call(
        functools.partial(_rwkv_kernel, npair=npair, nch=tb // RWKV_CHUNK),
        grid=(2, b, c // lanes, nblk),
        in_specs=[shared, shared, shared, direc, direc, direc],
        out_specs=out,
        out_shape=jax.ShapeDtypeStruct((2, b, tt - ctx_len, c), F32),
        scratch_shapes=[pltpu.VMEM((npair, PAIR, PAIR), F32)],
        compiler_params=pltpu.CompilerParams(
            dimension_semantics=("arbitrary", "arbitrary", "arbitrary", "arbitrary")),
        name="rwkv_scan",
        interpret=interpret,
    )(r, v, kk, ld, kd, bb)


def _time_block_map(nctx, nblk):
    def tmap(d, k):
        bwd = jnp.where(k < nctx, nctx - 1 - k, nblk - 1 - (k - nctx))
        return jnp.where(d == 1, bwd, k)
    return tmap


SSD_P = 64


def _ssd_kernel(xs_ref, bm_ref, cm_ref, dtc_ref, lac_ref, dtr_ref, lar_ref, y_ref, st_ref, *, chunk):
    is_bwd = pl.program_id(0) == 1
    blk = pl.program_id(3)

    @pl.when(blk == 0)
    def _():
        st_ref[...] = jnp.zeros_like(st_ref)

    sgn = jnp.where(is_bwd, -1, 1).astype(jnp.int32)
    l_i = lax.broadcasted_iota(jnp.int32, (chunk, chunk), 0)
    s_i = lax.broadcasted_iota(jnp.int32, (chunk, chunk), 1)
    incl = (l_i - s_i) * sgn >= 0
    tri = jnp.where(incl, 1.0, 0.0).astype(F32)
    lane = lax.broadcasted_iota(jnp.int32, (chunk, PAIR), 1)
    first = lane < SSD_P
    first_row = first[:1]

    ngb = dtc_ref.shape[2]
    nh = dtc_ref.shape[-1]
    nst = bm_ref.shape[-1] // ngb
    groups = list(range(ngb))
    pairs = [(gi, q) for gi in groups for q in range(nh // 2)]
    heads = [(gi, h) for gi in groups for h in range(nh)]
    pair_lanes = {(gi, q): slice((gi * nh // 2 + q) * PAIR, (gi * nh // 2 + q + 1) * PAIR)
                  for gi, q in pairs}

    bmat = [bm_ref[0, :, gi * nst:(gi + 1) * nst] for gi in groups]
    cmat_b = [cm_ref[0, :, gi * nst:(gi + 1) * nst].astype(BF16) for gi in groups]
    dtc = [dtc_ref[0, 0, gi] for gi in groups]
    lac = [lac_ref[0, 0, gi] for gi in groups]
    dtr = [dtr_ref[0, 0, gi] for gi in groups]
    lar = [lar_ref[0, 0, gi] for gi in groups]

    cum_col = _each(lambda z: _mm(tri, z, HI), lac)
    cum_row = _each(lambda z: _mm_nt(z, tri, HI), lar)
    cb = _each(lambda c, b: _mm_nt(c, b.astype(BF16)), cmat_b, bmat)
    b_t = _each(lambda b: b.T.astype(BF16), bmat)
    tot = _each(lambda z: jnp.sum(z, axis=0, keepdims=True), lac)
    coef = _each(lambda t, c, z: jnp.exp(t - c) * z, tot, cum_col, dtc)
    ecum = _each(jnp.exp, cum_col)
    etot = _each(jnp.exp, tot)

    xp = [xs_ref[0, :, pair_lanes[p]] for p in pairs]
    xp_b = _each(lambda z: z.astype(BF16), xp)
    st = [st_ref[:, pair_lanes[p]] for p in pairs]
    wmat = [(cb[gi] * jnp.exp(jnp.where(incl, cum_col[gi][:, h:h + 1] - cum_row[gi][h:h + 1, :],
                                        -jnp.inf)) * dtr[gi][h:h + 1, :]).astype(BF16)
            for gi, h in heads]
    yh = [_mm(w, xp_b[i // 2]) for i, w in enumerate(wmat)]
    yoff = [_mm(cmat_b[gi], s.astype(BF16)) for (gi, q), s in zip(pairs, st)]
    snew = [_mm(b_t[gi], (x * jnp.where(first, coef[gi][:, 2 * q:2 * q + 1],
                                        coef[gi][:, 2 * q + 1:2 * q + 2])).astype(BF16))
            for (gi, q), x in zip(pairs, xp)]
    for i, (gi, q) in enumerate(pairs):
        h0, h1 = 2 * q, 2 * q + 1
        y = jnp.where(first, yh[2 * i], yh[2 * i + 1])
        y = y + yoff[i] * jnp.where(first, ecum[gi][:, h0:h0 + 1], ecum[gi][:, h1:h1 + 1])
        st_ref[:, pair_lanes[(gi, q)]] = (
            st[i] * jnp.where(first_row, etot[gi][:, h0:h0 + 1], etot[gi][:, h1:h1 + 1]) + snew[i])
        y_ref[0, 0, :, pair_lanes[(gi, q)]] = y


def ssd_scan(xbc, dtc, lac, dtr, lar, *, ctx_len, nst, chunk=256, ngb=4, interpret=False):
    b, tt, _ = xbc.shape
    ng, nh = dtc.shape[2], dtc.shape[4]
    gw = ngb * nh * SSD_P
    width = ng * nh * SSD_P
    bw = ngb * nst
    assert xbc.shape[2] == width + 2 * ng * nst and width % bw == 0 and ng % ngb == 0
    assert ctx_len % chunk == 0 and tt % chunk == 0 and nh % 2 == 0
    nctx, nblk = ctx_len // chunk, tt // chunk
    tmap = _time_block_map(nctx, nblk)
    b0 = width // bw
    c0 = b0 + ng // ngb
    return pl.pallas_call(
        functools.partial(_ssd_kernel, chunk=chunk),
        grid=(2, b, ng // ngb, nblk),
        in_specs=[
            pl.BlockSpec((1, chunk, gw), lambda d, bi, g, k: (bi, tmap(d, k), g)),
            pl.BlockSpec((1, chunk, bw), lambda d, bi, g, k: (bi, tmap(d, k), b0 + g)),
            pl.BlockSpec((1, chunk, bw), lambda d, bi, g, k: (bi, tmap(d, k), c0 + g)),
            pl.BlockSpec((1, 1, ngb, chunk, nh), lambda d, bi, g, k: (d, bi, g, tmap(d, k), 0)),
            pl.BlockSpec((1, 1, ngb, chunk, nh), lambda d, bi, g, k: (d, bi, g, tmap(d, k), 0)),
            pl.BlockSpec((1, 1, ngb, nh, chunk), lambda d, bi, g, k: (d, bi, g, 0, tmap(d, k))),
            pl.BlockSpec((1, 1, ngb, nh, chunk), lambda d, bi, g, k: (d, bi, g, 0, tmap(d, k))),
        ],
        out_specs=pl.BlockSpec(
            (1, 1, chunk, gw),
            lambda d, bi, g, k: (d, bi, tmap(d, jnp.maximum(k, nctx)) - nctx, g)),
        out_shape=jax.ShapeDtypeStruct((2, b, tt - ctx_len, width), F32),
        scratch_shapes=[pltpu.VMEM((nst, gw), F32)],
        compiler_params=pltpu.CompilerParams(
            dimension_semantics=("arbitrary", "arbitrary", "arbitrary", "arbitrary")),
        name="ssd_scan",
        interpret=interpret,
    )(xbc, xbc, xbc, dtc, lac, dtr, lar)


RMS_EPS = 1e-6
VMEM_LIMIT = 56 * 1024 * 1024


def _mod_kernel(s_ref, w_ref, b_ref, o_ref):
    s = s_ref[...]
    s = s * jax.nn.sigmoid(s)
    o_ref[...] = _mm(s, w_ref[...], HI) + b_ref[...]


def mod_matmul(s, w, bias, *, tn=512, interpret=False):
    m, k = s.shape
    n = w.shape[1]
    return pl.pallas_call(
        _mod_kernel,
        grid=(n // tn,),
        in_specs=[pl.BlockSpec((m, k), lambda j: (0, 0)),
                  pl.BlockSpec((k, tn), lambda j: (0, j)),
                  pl.BlockSpec((1, tn), lambda j: (0, j))],
        out_specs=pl.BlockSpec((m, tn), lambda j: (0, j)),
        out_shape=jax.ShapeDtypeStruct((m, n), F32),
        compiler_params=pltpu.CompilerParams(dimension_semantics=("arbitrary",),
                                             vmem_limit_bytes=VMEM_LIMIT),
        name="mod_matmul",
        interpret=interpret,
    )(s, w, bias.reshape(1, n))


def _norm_modulate_kernel(a_ref, b_ref, g_ref, sh_ref, sc_ref, o_ref, *, n_first):
    def emit(h_ref):
        h = h_ref[...]
        hn = h * lax.rsqrt(jnp.mean(h * h, axis=-1, keepdims=True) + RMS_EPS)
        o_ref[...] = ((hn * g_ref[...]) * (1.0 + sc_ref[0]) + sh_ref[0]).astype(o_ref.dtype)

    is_first = pl.program_id(0) < n_first
    pl.when(is_first)(lambda: emit(a_ref))
    pl.when(jnp.logical_not(is_first))(lambda: emit(b_ref))


def norm_modulate(first, second, g, mod3, *, row_of_block, shift_idx, scale_idx, n_mod, tm=256,
                  interpret=False):
    d = second.shape[1]
    n_first = 0 if first is None else first.shape[0] // tm
    a = second if first is None else first
    rows = n_first * tm + second.shape[0]
    assert second.shape[0] % tm == 0 and a.shape[0] % tm == 0
    last_a = a.shape[0] // tm - 1
    return pl.pallas_call(
        functools.partial(_norm_modulate_kernel, n_first=n_first),
        grid=(rows // tm,),
        in_specs=[
            pl.BlockSpec((tm, d), lambda i: (jnp.minimum(i, last_a), 0)),
            pl.BlockSpec((tm, d), lambda i: (jnp.maximum(i - n_first, 0), 0)),
            pl.BlockSpec((1, d), lambda i: (0, 0)),
            pl.BlockSpec((1, 1, d), lambda i: (row_of_block(i) * n_mod + shift_idx, 0, 0)),
            pl.BlockSpec((1, 1, d), lambda i: (row_of_block(i) * n_mod + scale_idx, 0, 0)),
        ],
        out_specs=pl.BlockSpec((tm, d), lambda i: (i, 0)),
        out_shape=jax.ShapeDtypeStruct((rows, d), BF16),
        compiler_params=pltpu.CompilerParams(dimension_semantics=("arbitrary",),
                                             vmem_limit_bytes=VMEM_LIMIT),
        name="norm_modulate",
        interpret=interpret,
    )(a, second, g.reshape(1, d), mod3, mod3)


def _matmul_kernel(x_ref, w_ref, o_ref):
    o_ref[...] = _mm(x_ref[...], w_ref[...])


def matmul(x, w, *, tm, tn, interpret=False):
    rows, k = x.shape
    n = w.shape[1]
    assert rows % tm == 0 and n % tn == 0
    return pl.pallas_call(
        _matmul_kernel,
        grid=(rows // tm, n // tn),
        in_specs=[pl.BlockSpec((tm, k), lambda i, j: (i, 0)),
                  pl.BlockSpec((k, tn), lambda i, j: (0, j))],
        out_specs=pl.BlockSpec((tm, tn), lambda i, j: (i, j)),
        out_shape=jax.ShapeDtypeStruct((rows, n), F32),
        compiler_params=pltpu.CompilerParams(dimension_semantics=("arbitrary", "arbitrary"),
                                             vmem_limit_bytes=VMEM_LIMIT),
        name="matmul",
        interpret=interpret,
    )(x, w)


def _matmul_gated_resid_kernel(ya_ref, yb_ref, w_ref, x_ref, gt_ref, o_ref):
    ka = ya_ref.shape[1]
    acc = _mm(ya_ref[...], w_ref[:ka, :]) + _mm(yb_ref[...], w_ref[ka:, :])
    o_ref[...] = x_ref[...] + gt_ref[0] * acc


def matmul_gated_resid(ya, yb, w, resid, mod3, *, row_of_block, gate_idx, n_mod, tm, tn,
                       interpret=False):
    rows, ka = ya.shape
    kb = yb.shape[1]
    k, n = w.shape
    assert rows % tm == 0 and n % tn == 0 and ka + kb == k
    return pl.pallas_call(
        _matmul_gated_resid_kernel,
        grid=(rows // tm, n // tn),
        in_specs=[
            pl.BlockSpec((tm, ka), lambda i, j: (i, 0)),
            pl.BlockSpec((tm, kb), lambda i, j: (i, 0)),
            pl.BlockSpec((k, tn), lambda i, j: (0, j)),
            pl.BlockSpec((tm, tn), lambda i, j: (i, j)),
            pl.BlockSpec((1, 1, tn), lambda i, j: (row_of_block(i) * n_mod + gate_idx, 0, j)),
        ],
        out_specs=pl.BlockSpec((tm, tn), lambda i, j: (i, j)),
        out_shape=jax.ShapeDtypeStruct((rows, n), F32),
        compiler_params=pltpu.CompilerParams(dimension_semantics=("arbitrary", "arbitrary"),
                                             vmem_limit_bytes=VMEM_LIMIT),
        name="matmul_gated_resid",
        interpret=interpret,
    )(ya, yb, w, resid, mod3)


def _top_values(s, k):
    n = s.shape[0]
    row = lax.broadcasted_iota(jnp.int32, s.shape, 0)
    rank = jnp.full(s.shape, float(k), F32)
    vals = []
    for a in range(k):
        mx = jnp.max(s, axis=0, keepdims=True)
        first = jnp.min(jnp.where(s == mx, row, n), axis=0, keepdims=True)
        hit = row == first
        s = jnp.where(hit, -jnp.inf, s)
        rank = jnp.where(hit, float(a), rank)
        vals.append(mx)
    return jnp.concatenate(vals, axis=0), rank


def _peer_score_kernel(q_ref, k1_ref, k2_ref, cnt_ref, rank_ref, e1_ref, e2_ref, *, topk):
    nheads, nkeys, qd = k1_ref.shape
    for h in range(nheads):
        q1 = q_ref[:, h * 2 * qd:(h * 2 + 1) * qd]
        q2 = q_ref[:, (h * 2 + 1) * qd:(h * 2 + 2) * qd]
        s1 = _mm_nt(k1_ref[h], q1, HI)
        s2 = _mm_nt(k2_ref[h], q2, HI)
        v1, rank1 = _top_values(s1, topk)
        v2, rank2 = _top_values(s2, topk)
        cand = jnp.concatenate([v1[a:a + 1] + v2[:topk // (a + 1)] for a in range(topk)], axis=0)
        best, _ = _top_values(cand, topk)
        tau = best[topk - 1:topk]
        z = jnp.sum(jnp.exp(best - best[:1]), axis=0, keepdims=True)
        cnt = jnp.zeros_like(s1)
        for a in range(topk):
            cnt_a = jnp.sum(jnp.where(v1[a:a + 1] + v2 >= tau, 1.0, 0.0), axis=0, keepdims=True)
            cnt = jnp.where(rank1 == float(a), cnt_a, cnt)
        cnt_ref[h] = cnt.astype(cnt_ref.dtype)
        rank_ref[h] = rank2.astype(rank_ref.dtype)
        e1_ref[h] = jnp.exp(s1 - v1[:1]).astype(e1_ref.dtype)
        e2_ref[h] = (jnp.exp(s2 - v2[:1]) / z).astype(e2_ref.dtype)


def peer_scores(q, k1, k2, *, topk, tb=256, gate_dtype=BF16, interpret=False):
    t = q.shape[0]
    nheads, nkeys, qd = k1.shape
    big = pl.BlockSpec((nheads, nkeys, tb), lambda i: (0, 0, i))
    shape_of = lambda dt: jax.ShapeDtypeStruct((nheads, nkeys, t), dt)
    return pl.pallas_call(
        functools.partial(_peer_score_kernel, topk=topk),
        grid=(t // tb,),
        in_specs=[pl.BlockSpec((tb, q.shape[1]), lambda i: (i, 0)),
                  pl.BlockSpec(k1.shape, lambda i: (0, 0, 0)),
                  pl.BlockSpec(k2.shape, lambda i: (0, 0, 0))],
        out_specs=[big, big, big, big],
        out_shape=[shape_of(F32), shape_of(gate_dtype), shape_of(F32), shape_of(gate_dtype)],
        compiler_params=pltpu.CompilerParams(dimension_semantics=("arbitrary",),
                                             vmem_limit_bytes=VMEM_LIMIT),
        name="peer_scores",
        interpret=interpret,
    )(q, k1, k2)


def _peer_weight_kernel(u_ref, x_ref, cnt_ref, rank_ref, e1_ref, e2_ref, w_ref):
    nheads, nkeys, tb = rank_ref.shape
    gdt = rank_ref.dtype
    sub = 8 * 4 // jnp.dtype(gdt).itemsize
    a = _mm_nt(u_ref[...], x_ref[...])
    act = 0.5 * a * (1.0 + lax.erf(a * (1.0 / math.sqrt(2.0))))

    def rows_of(ref, h, ii):
        row = jnp.broadcast_to(ref[h, ii:ii + 1, :], (sub, tb)).astype(gdt)
        return jnp.concatenate([row] * (nkeys // sub), axis=0)

    for ii in range(u_ref.shape[0] // nkeys):
        gate = jnp.zeros((nkeys, tb), gdt)
        for h in range(nheads):
            sel = rank_ref[h] < rows_of(cnt_ref, h, ii)
            gate = gate + jnp.where(sel, e2_ref[h], jnp.zeros((), gdt)) * rows_of(e1_ref, h, ii)
        rows = slice(ii * nkeys, (ii + 1) * nkeys)
        w_ref[rows, :] = (act[rows].astype(gdt) * gate).astype(w_ref.dtype)


def peer_weights(u, xn, cnt, rank, e1, e2, *, te=1024, tb=512, interpret=False):
    ne, d = u.shape
    t = xn.shape[0]
    nheads, nkeys, _ = cnt.shape
    ni = te // nkeys
    s1_spec = pl.BlockSpec((nheads, ni, tb), lambda i, j: (0, j, i))
    s2_spec = pl.BlockSpec((nheads, nkeys, tb), lambda i, j: (0, 0, i))
    return pl.pallas_call(
        _peer_weight_kernel,
        grid=(t // tb, ne // te),
        in_specs=[pl.BlockSpec((te, d), lambda i, j: (j, 0)),
                  pl.BlockSpec((tb, d), lambda i, j: (i, 0)),
                  s1_spec, s2_spec, s1_spec, s2_spec],
        out_specs=pl.BlockSpec((te, tb), lambda i, j: (j, i)),
        out_shape=jax.ShapeDtypeStruct((ne, t), BF16),
        compiler_params=pltpu.CompilerParams(dimension_semantics=("arbitrary", "arbitrary"),
                                             vmem_limit_bytes=VMEM_LIMIT),
        name="peer_weights",
        interpret=interpret,
    )(u, xn, cnt, rank, e1, e2)


def _peer_out_kernel(w_ref, v_ref, o_ref):
    @pl.when(pl.program_id(1) == 0)
    def _():
        o_ref[...] = jnp.zeros_like(o_ref)

    o_ref[...] += lax.dot_general(w_ref[...], v_ref[...], (((0,), (0,)), ((), ())),
                                  preferred_element_type=F32)


def peer_out(wt, v, *, tb=512, te=1024, interpret=False):
    ne, t = wt.shape
    d = v.shape[1]
    return pl.pallas_call(
        _peer_out_kernel,
        grid=(t // tb, ne // te),
        in_specs=[pl.BlockSpec((te, tb), lambda i, j: (j, i)),
                  pl.BlockSpec((te, d), lambda i, j: (j, 0))],
        out_specs=pl.BlockSpec((tb, d), lambda i, j: (i, 0)),
        out_shape=jax.ShapeDtypeStruct((t, d), F32),
        compiler_params=pltpu.CompilerParams(dimension_semantics=("arbitrary", "arbitrary"),
                                             vmem_limit_bytes=VMEM_LIMIT),
        name="peer_out",
        interpret=interpret,
    )(wt, v)


def _final_kernel(h_ref, p_ref, gt_ref, g_ref, o_ref):
    h = h_ref[...] + gt_ref[0] * p_ref[...]
    o_ref[...] = h * lax.rsqrt(jnp.mean(h * h, axis=-1, keepdims=True) + RMS_EPS) * g_ref[...]


def final_norm(h, p, mod3, g, *, row_of_block, gate_idx, n_mod, tm=256, interpret=False):
    rows, d = h.shape
    blk = pl.BlockSpec((tm, d), lambda i: (i, 0))
    return pl.pallas_call(
        _final_kernel,
        grid=(rows // tm,),
        in_specs=[blk, blk,
                  pl.BlockSpec((1, 1, d), lambda i: (row_of_block(i) * n_mod + gate_idx, 0, 0)),
                  pl.BlockSpec((1, d), lambda i: (0, 0))],
        out_specs=blk,
        out_shape=jax.ShapeDtypeStruct((rows, d), F32),
        compiler_params=pltpu.CompilerParams(dimension_semantics=("arbitrary",),
                                             vmem_limit_bytes=VMEM_LIMIT),
        name="final_norm",
        interpret=interpret,
    )(h, p, mod3, g.reshape(1, d))


N_MOD = 6
GRID_W = 64
RWKV_GN_EPS = 64e-5
SSM_STATE = 128
PEER_TOPK = 16


LANES = 128
TOK_BLK = 256


def _block_ones(group):
    i = lax.broadcasted_iota(jnp.int32, (LANES, LANES), 0) // group
    j = lax.broadcasted_iota(jnp.int32, (LANES, LANES), 1) // group
    return jnp.where(i == j, 1.0, 0.0).astype(BF16)


def _ones_mm(x, ones):
    hi = x.astype(BF16)
    lo = (x - hi.astype(F32)).astype(BF16)
    return _mm(hi, ones) + _mm(lo, ones)


def _segsum(x, ones):
    return jnp.concatenate([_ones_mm(x[:, i * LANES:(i + 1) * LANES], ones)
                            for i in range(x.shape[1] // LANES)], axis=1)


def _softplus(x):
    return jnp.maximum(x, 0.0) + jnp.log(1.0 + jnp.exp(-jnp.abs(x)))


def _shift_mix(cur_ref, prev_ref, next_ref, mu_ref, is_ctx, top, bottom, grid_w):
    cur = cur_ref[...]
    n, w = cur.shape
    row = lax.broadcasted_iota(jnp.int32, (n, w), 0)
    lane = lax.broadcasted_iota(jnp.int32, (n, w), 1)
    before = pltpu.roll(cur, 1, axis=0)
    after = pltpu.roll(cur, n - 1, axis=0)
    col = row % grid_w
    left = jnp.where(col == 0, 0.0, before)
    right = jnp.where(col == grid_w - 1, 0.0, after)
    up = jnp.concatenate([prev_ref[...] * (1.0 - top), cur[:n - grid_w]], axis=0)
    down = jnp.concatenate([cur[grid_w:], next_ref[...] * (1.0 - bottom)], axis=0)
    slot = lane % 4
    sh_grid = jnp.where(slot == 0, left, jnp.where(slot == 1, right, jnp.where(slot == 2, up, down)))
    sh_seq = jnp.where(lane % 2 == 0, jnp.where(row == 0, 0.0, before),
                       jnp.where(row == n - 1, 0.0, after))
    shifted = jnp.where(is_ctx, sh_seq, sh_grid)
    return cur + mu_ref[...] * (shifted - cur)


def _rwkv_prep_kernel(r_c, r_p, r_n, k_c, k_p, k_n, v_c, v_p, v_n, l_c, l_p, l_n,
                      mu_r, mu_k, mu_v, mu_l, w2_ref, a2_ref, g2_ref, w0_ref, a0_ref,
                      kk_ref, ka_ref, rk_ref,
                      r_o, v_o, kkn_o, ld_o, kd_o, bb_o, bonus_o, gate_o,
                      *, nctx_blk, blk_per_seq, grid_w):
    i = pl.program_id(0)
    is_ctx = i < nctx_blk
    tblk = (i - nctx_blk) % blk_per_seq
    top = jnp.where(tblk == 0, 1.0, 0.0)
    bottom = jnp.where(tblk == blk_per_seq - 1, 1.0, 0.0)
    mix = functools.partial(_shift_mix, is_ctx=is_ctx, top=top, bottom=bottom, grid_w=grid_w)
    r = mix(r_c, r_p, r_n, mu_r)
    k = mix(k_c, k_p, k_n, mu_k)
    v = mix(v_c, v_p, v_n, mu_v)
    low = mix(l_c, l_p, l_n, mu_l)
    cw = r.shape[1]

    w_low = _mm(jnp.tanh(low).astype(BF16), w2_ref[...])
    a_low = _mm(low.astype(BF16), a2_ref[...])
    gate_o[0] = _mm(jax.nn.sigmoid(low).astype(BF16), g2_ref[...])

    ones = _block_ones(RWKV_HEAD)
    kraw = k * kk_ref[...]
    kkn = kraw * lax.rsqrt(jnp.maximum(_segsum(kraw * kraw, ones), 1e-24))
    ksum = jnp.zeros_like(k)
    for d in range(2):
        cols = slice(d * cw, (d + 1) * cw)
        w = -_softplus(-(w0_ref[:, cols] + w_low[:, cols])) - 0.5
        a = jax.nn.sigmoid(a0_ref[:, cols] + a_low[:, cols])
        k_dir = k * (1.0 + (a - 1.0) * ka_ref[...])
        ld_o[d, 0] = -jnp.exp(w)
        kd_o[d, 0] = k_dir
        bb_o[d, 0] = kkn * a
        ksum = ksum + k_dir
    r_o[0] = r
    v_o[0] = v
    kkn_o[0] = kkn
    bonus_o[0] = _segsum(r * ksum * rk_ref[...], ones) * v


def rwkv_prep(proj, mu_p, w2g, a2g, g2g, w0g, a0g, k_k, k_a, r_k, *, nb, cl, t, cw, lr_off, lr_w,
              grid_w, cg=512, interpret=False):
    tb = TOK_BLK
    assert cl == tb and t % tb == 0 and tb % grid_w == 0 and cw % cg == 0 and lr_off % lr_w == 0
    nctx_blk = nb
    blk_per_seq = t // tb
    nblk = nctx_blk + nb * blk_per_seq
    hb = tb // grid_w
    last_h = nblk * hb - 1
    ncg = cw // cg

    def seg(col0, width):
        cb = col0 // width
        return [pl.BlockSpec((tb, width), lambda i, g, cb=cb, s=(width == cg): (i, cb + g * s)),
                pl.BlockSpec((grid_w, width),
                             lambda i, g, cb=cb, s=(width == cg): (jnp.maximum(i * hb - 1, 0), cb + g * s)),
                pl.BlockSpec((grid_w, width),
                             lambda i, g, cb=cb, s=(width == cg): (jnp.minimum((i + 1) * hb, last_h), cb + g * s))]

    def mu_spec(col0, width):
        cb = col0 // width
        return pl.BlockSpec((1, width), lambda i, g, cb=cb, s=(width == cg): (0, cb + g * s))

    def tok(i):
        b = jnp.where(i < nctx_blk, i, (i - nctx_blk) // blk_per_seq)
        tk = jnp.where(i < nctx_blk, 0, 1 + (i - nctx_blk) % blk_per_seq)
        return b, tk

    def out1(i, g):
        b, tk = tok(i)
        return (b, tk, g)

    def out2(i, g):
        b, tk = tok(i)
        return (0, b, tk, g)

    per_c = pl.BlockSpec((1, cg), lambda i, g: (0, g))
    one_shape = jax.ShapeDtypeStruct((nb, cl + t, cw), F32)
    two_shape = jax.ShapeDtypeStruct((2, nb, cl + t, cw), F32)
    one_spec = pl.BlockSpec((1, tb, cg), out1)
    two_spec = pl.BlockSpec((2, 1, tb, cg), out2)
    return pl.pallas_call(
        functools.partial(_rwkv_prep_kernel, nctx_blk=nctx_blk, blk_per_seq=blk_per_seq,
                          grid_w=grid_w),
        grid=(nblk, ncg),
        in_specs=(seg(0, cg) + seg(cw, cg) + seg(2 * cw, cg) + seg(lr_off, lr_w)
                  + [mu_spec(0, cg), mu_spec(cw, cg), mu_spec(2 * cw, cg), mu_spec(lr_off, lr_w)]
                  + [pl.BlockSpec((lr_w, 2 * cg), lambda i, g: (0, g)),
                     pl.BlockSpec((lr_w, 2 * cg), lambda i, g: (0, g)),
                     pl.BlockSpec((lr_w, cg), lambda i, g: (0, g)),
                     pl.BlockSpec((1, 2 * cg), lambda i, g: (0, g)),
                     pl.BlockSpec((1, 2 * cg), lambda i, g: (0, g)),
                     per_c, per_c, per_c]),
        out_specs=[one_spec, one_spec, one_spec, two_spec, two_spec, two_spec, one_spec, one_spec],
        out_shape=[one_shape] * 3 + [two_shape] * 3 + [one_shape] * 2,
        compiler_params=pltpu.CompilerParams(dimension_semantics=("arbitrary", "arbitrary"),
                                             vmem_limit_bytes=VMEM_LIMIT),
        name="rwkv_prep",
        interpret=interpret,
    )(*([proj] * 12), mu_p, mu_p, mu_p, mu_p, w2g, a2g, g2g, w0g, a0g,
      k_k.reshape(1, cw), k_a.reshape(1, cw), r_k.reshape(1, cw))


def _ssd_prep_kernel(c_ref, p_ref, n_ref, w_ref, b_ref, o_ref, *, nctx_blk, blk_per_seq):
    i = pl.program_id(0)
    is_ctx = i < nctx_blk
    tblk = (i - nctx_blk) % blk_per_seq
    first = jnp.logical_or(is_ctx, tblk == 0)
    last = jnp.logical_or(is_ctx, tblk == blk_per_seq - 1)
    cur = c_ref[...]
    n = cur.shape[0]
    row = lax.broadcasted_iota(jnp.int32, cur.shape, 0)
    halo = p_ref.shape[0]
    prev_row = p_ref[halo - 1:halo, :] * jnp.where(first, 0.0, 1.0)
    next_row = n_ref[0:1, :] * jnp.where(last, 0.0, 1.0)
    before = jnp.where(row == 0, prev_row, pltpu.roll(cur, 1, axis=0))
    after = jnp.where(row == n - 1, next_row, pltpu.roll(cur, n - 1, axis=0))
    y = w_ref[0:1, :] * before + w_ref[1:2, :] * cur + w_ref[2:3, :] * after + b_ref[...]
    o_ref[0] = y * jax.nn.sigmoid(y)


def ssd_prep(proj, conv_w, conv_b, *, nb, cl, t, col0, cg=512, interpret=False):
    tb = TOK_BLK
    halo = 8
    xbc_w = conv_w.shape[1]
    assert conv_w.shape[0] == 3 and cl == tb and t % tb == 0 and col0 % cg == 0 and xbc_w % cg == 0
    nctx_blk = nb
    blk_per_seq = t // tb
    nblk = nctx_blk + nb * blk_per_seq
    hb = tb // halo
    last_h = nblk * hb - 1
    cb = col0 // cg

    def out_map(i, g):
        b = jnp.where(i < nctx_blk, i, (i - nctx_blk) // blk_per_seq)
        tk = jnp.where(i < nctx_blk, 0, 1 + (i - nctx_blk) % blk_per_seq)
        return (b, tk, g)

    return pl.pallas_call(
        functools.partial(_ssd_prep_kernel, nctx_blk=nctx_blk, blk_per_seq=blk_per_seq),
        grid=(nblk, xbc_w // cg),
        in_specs=[pl.BlockSpec((tb, cg), lambda i, g: (i, cb + g)),
                  pl.BlockSpec((halo, cg), lambda i, g: (jnp.maximum(i * hb - 1, 0), cb + g)),
                  pl.BlockSpec((halo, cg), lambda i, g: (jnp.minimum((i + 1) * hb, last_h), cb + g)),
                  pl.BlockSpec((3, cg), lambda i, g: (0, g)),
                  pl.BlockSpec((1, cg), lambda i, g: (0, g))],
        out_specs=pl.BlockSpec((1, tb, cg), out_map),
        out_shape=jax.ShapeDtypeStruct((nb, cl + t, xbc_w), F32),
        compiler_params=pltpu.CompilerParams(dimension_semantics=("arbitrary", "arbitrary"),
                                             vmem_limit_bytes=VMEM_LIMIT),
        name="ssd_prep",
        interpret=interpret,
    )(proj, proj, proj, conv_w, conv_b.reshape(1, xbc_w))


def _mixer_finish_kernel(o_ref, bonus_ref, gate_ref, lnw_ref, lnb_ref,
                         ys_ref, xs_ref, z_ref, dsk_ref, nw_ref, yr_ref, ysd_ref, *, ssd_group):
    o = o_ref[0, 0] + o_ref[1, 0]
    ones = _block_ones(RWKV_HEAD)
    mean = _segsum(o, ones) * (1.0 / RWKV_HEAD)
    cen = o - mean
    var = _segsum(cen * cen, ones) * (1.0 / RWKV_HEAD)
    on = cen * lax.rsqrt(var + RWKV_GN_EPS)
    yr_ref[...] = ((on * lnw_ref[...] + lnb_ref[...] + bonus_ref[0]) * gate_ref[0]).astype(yr_ref.dtype)

    z = z_ref[...]
    y = (ys_ref[0, 0] + ys_ref[1, 0] + dsk_ref[...] * xs_ref[0]) * (z * jax.nn.sigmoid(z))
    sq = y * y
    all_ones = jnp.ones((LANES, LANES), BF16)
    tiles = ssd_group // LANES
    parts = []
    for gi in range(y.shape[1] // ssd_group):
        acc = sq[:, gi * ssd_group:gi * ssd_group + LANES]
        for ti in range(1, tiles):
            acc = acc + sq[:, gi * ssd_group + ti * LANES:gi * ssd_group + (ti + 1) * LANES]
        ms = _ones_mm(acc, all_ones) * (1.0 / ssd_group)
        parts.extend([ms] * tiles)
    ms = jnp.concatenate(parts, axis=1)
    ysd_ref[...] = (y * lax.rsqrt(ms + RMS_EPS) * nw_ref[...]).astype(ysd_ref.dtype)


def mixer_finish(o, bonus, gate, ln_w, ln_b, ys, xbc, proj, d_skip, norm_w, *, nb, cl, t, z_col0,
                 ssd_group, cg=512, interpret=False):
    tb = TOK_BLK
    cw = o.shape[-1]
    assert cl % tb == 0 and t % tb == 0 and z_col0 % cg == 0 and cg % ssd_group == 0
    off = cl // tb
    bps = t // tb
    zrow0 = nb * cl // tb
    zc = z_col0 // cg
    two = pl.BlockSpec((2, 1, tb, cg), lambda b, k, g: (0, b, k, g))
    shifted = pl.BlockSpec((1, tb, cg), lambda b, k, g: (b, k + off, g))
    per_c = pl.BlockSpec((1, cg), lambda b, k, g: (0, g))
    out = pl.BlockSpec((tb, cg), lambda b, k, g: (b * bps + k, g))
    return pl.pallas_call(
        functools.partial(_mixer_finish_kernel, ssd_group=ssd_group),
        grid=(nb, bps, cw // cg),
        in_specs=[two, shifted, shifted, per_c, per_c, two, shifted,
                  pl.BlockSpec((tb, cg), lambda b, k, g: (zrow0 + b * bps + k, zc + g)),
                  per_c, per_c],
        out_specs=[out, out],
        out_shape=[jax.ShapeDtypeStruct((nb * t, cw), BF16)] * 2,
        compiler_params=pltpu.CompilerParams(
            dimension_semantics=("arbitrary", "arbitrary", "arbitrary"),
            vmem_limit_bytes=VMEM_LIMIT),
        name="mixer_finish",
        interpret=interpret,
    )(o, bonus, gate, ln_w.reshape(1, cw), ln_b.reshape(1, cw), ys, xbc, proj,
      d_skip.reshape(1, cw), norm_w.reshape(1, cw))


def _col_form(zz, ng):
    b, tt, _, nh = zz.shape
    return jnp.transpose(zz.reshape(b, tt, 2, ng, nh // ng), (2, 0, 3, 1, 4))


def _row_form(zz, ng):
    b, tt, _, nh = zz.shape
    return jnp.transpose(zz.reshape(b, tt, 2, ng, nh // ng), (2, 0, 3, 4, 1))


COL_BLK = 512
PROJ_ROWS = 1024
LR_BLK = 768


def _by_group(z, ncg):
    rows, _, c = z.shape
    return jnp.transpose(z.reshape(rows, 2, ncg, c // ncg), (0, 2, 1, 3)).reshape(rows, 2 * c)


def _in_proj_layout(w_in, rwkv_mu, w2, a2, g2, w0, a0, cw, sw, xbc_w, nh):
    d = w_in.shape[0]
    rc = rwkv_mu.shape[0]
    lr = rc - 3 * cw
    dr, ir, gr = w2.shape[1], a2.shape[1], g2.shape[0]
    assert lr == 2 * dr + 2 * ir + gr and lr <= LR_BLK and (3 * cw) % LR_BLK == 0
    lr_off = 3 * cw
    z_off = -(-(lr_off + LR_BLK) // COL_BLK) * COL_BLK
    xbc_off = z_off + sw
    dt_off = xbc_off + xbc_w
    total = -(-(dt_off + 2 * nh) // COL_BLK) * COL_BLK
    zc = lambda n: jnp.zeros((d, n), w_in.dtype)
    w_p = jnp.concatenate([
        w_in[:, :rc], zc(z_off - rc), w_in[:, rc:rc + sw + xbc_w + 2 * nh],
        zc(total - dt_off - 2 * nh)], axis=1).astype(BF16)
    mu_w = -(-(lr_off + LR_BLK) // (3 * COL_BLK)) * 3 * COL_BLK
    mu_p = jnp.pad(rwkv_mu, (0, mu_w - rc)).reshape(1, mu_w)
    ncg = cw // COL_BLK
    wd = jnp.zeros((LR_BLK, 2, cw), F32).at[0:dr, 0].set(w2[0]).at[dr:2 * dr, 1].set(w2[1])
    ad = (jnp.zeros((LR_BLK, 2, cw), F32).at[2 * dr:2 * dr + ir, 0].set(a2[0])
          .at[2 * dr + ir:2 * dr + 2 * ir, 1].set(a2[1]))
    gd = jnp.zeros((LR_BLK, cw), F32).at[2 * dr + 2 * ir:lr].set(g2)
    return dict(w=w_p, mu=mu_p, lr_off=lr_off, z_off=z_off, xbc_off=xbc_off, dt_off=dt_off,
                w2g=_by_group(wd, ncg).astype(BF16), a2g=_by_group(ad, ncg).astype(BF16),
                g2g=gd.astype(BF16), w0g=_by_group(w0[None], ncg), a0g=_by_group(a0[None], ncg))


def kernel(x, c, ctx, c_ctx, w_mod, b_mod, norm1_g, norm2_g, w_in, w_out, rwkv_mu, rwkv_w0, rwkv_w2, rwkv_a0, rwkv_a2, rwkv_g2, rwkv_k_k, rwkv_k_a, rwkv_r_k, rwkv_ln_w, rwkv_ln_b, ssm_conv_w, ssm_conv_b, ssm_dt_bias, ssm_a_log, ssm_d, ssm_norm_w, peer_wq, peer_k1, peer_k2, peer_u, peer_v, final_g):
    assert w_in.shape[0] == 1, "single-layer block"
    nb, t, d = x.shape
    cl = ctx.shape[1]
    mod_rows = 8
    s = jnp.concatenate([c, c_ctx[None], jnp.zeros((mod_rows - nb - 1, d), F32)], axis=0)
    mod3 = mod_matmul(s, w_mod[0], b_mod[0]).reshape(mod_rows * N_MOD, 1, d)

    tm = 512
    x_blk = t // tm
    tn_blk = TOK_BLK
    nctx_blk = nb * cl // tn_blk
    xn1 = norm_modulate(
        ctx.reshape(nb * cl, d), x.reshape(nb * t, d), norm1_g[0], mod3,
        row_of_block=lambda i: jnp.where(i < nctx_blk, nb, (i - nctx_blk) // (t // tn_blk)),
        shift_idx=0, scale_idx=1, n_mod=N_MOD, tm=tn_blk)
    cw = rwkv_k_k.shape[1]
    sw = ssm_norm_w.shape[1]
    xbc_w = ssm_conv_w.shape[2]
    nh = ssm_a_log.shape[2]
    ng = (xbc_w - sw) // (2 * SSM_STATE)
    lay = _in_proj_layout(w_in[0], rwkv_mu[0], rwkv_w2[0], rwkv_a2[0], rwkv_g2[0], rwkv_w0[0],
                          rwkv_a0[0], cw, sw, xbc_w, nh)
    proj = matmul(xn1, lay["w"], tm=PROJ_ROWS, tn=COL_BLK)

    r, v, kk, ld, kd, bb, bonus, gate = rwkv_prep(
        proj, lay["mu"], lay["w2g"], lay["a2g"], lay["g2g"], lay["w0g"], lay["a0g"],
        rwkv_k_k[0], rwkv_k_a[0], rwkv_r_k[0].reshape(cw), nb=nb, cl=cl, t=t, cw=cw,
        lr_off=lay["lr_off"], lr_w=LR_BLK, grid_w=GRID_W, cg=COL_BLK)
    o = rwkv_scan(r, v, kk, ld, kd, bb, ctx_len=cl)

    xbc = ssd_prep(proj, ssm_conv_w[0], ssm_conv_b[0], nb=nb, cl=cl, t=t, col0=lay["xbc_off"],
                   cg=2 * COL_BLK)
    dt_raw = proj[:, lay["dt_off"]:lay["dt_off"] + 2 * nh]
    dt_raw = jnp.concatenate([dt_raw[:nb * cl].reshape(nb, cl, 2, nh),
                              dt_raw[nb * cl:].reshape(nb, t, 2, nh)], axis=1)
    dt = jax.nn.softplus(dt_raw + ssm_dt_bias[0])
    la = dt * -jnp.exp(ssm_a_log[0])
    ys = ssd_scan(xbc, _col_form(dt, ng), _col_form(la, ng), _row_form(dt, ng), _row_form(la, ng),
                  ctx_len=cl, nst=SSM_STATE)

    y_rwkv, y_ssd = mixer_finish(
        o, bonus, gate, rwkv_ln_w[0], rwkv_ln_b[0], ys, xbc, proj,
        jnp.repeat(ssm_d[0], sw // nh), ssm_norm_w[0], nb=nb, cl=cl, t=t, z_col0=lay["z_off"],
        ssd_group=sw // ng, cg=COL_BLK)
    by_batch = lambda i: i // x_blk
    h1 = matmul_gated_resid(y_rwkv, y_ssd, w_out[0].astype(BF16), x.reshape(nb * t, d), mod3,
                            row_of_block=by_batch, gate_idx=2, n_mod=N_MOD, tm=tm, tn=1024)

    xn2 = norm_modulate(None, h1, norm2_g[0], mod3, row_of_block=lambda i: i // (t // tn_blk),
                        shift_idx=3, scale_idx=4, n_mod=N_MOD, tm=tn_blk)
    q = matmul(xn2, peer_wq[0].astype(BF16), tm=PROJ_ROWS, tn=COL_BLK)
    cnt, rank, e1, e2 = peer_scores(q, peer_k1[0], peer_k2[0], topk=PEER_TOPK)
    wt = peer_weights(peer_u[0].astype(BF16), xn2, cnt, rank, e1, e2)
    p = peer_out(wt, peer_v[0].astype(BF16))
    tf = 256
    out = final_norm(h1, p, mod3, final_g, row_of_block=lambda i: i // (t // tf), gate_idx=5,
                     n_mod=N_MOD, tm=tf)
    return out.reshape(nb, t, d)
```

```python
import functools
import math

import jax
import jax.numpy as jnp
from jax import lax
from jax.experimental import pallas as pl
from jax.experimental.pallas import tpu as pltpu

F32 = jnp.float32
BF16 = jnp.bfloat16
HI = lax.Precision.HIGHEST

RWKV_CHUNK = 64
RWKV_HEAD = 64
PAIR = 2 * RWKV_HEAD


def _mm(a, b, precision=None):
    return lax.dot_general(a, b, (((1,), (0,)), ((), ())), precision=precision,
                           preferred_element_type=F32)


def _mm_nt(a, b, precision=None):
    return lax.dot_general(a, b, (((1,), (1,)), ((), ())), precision=precision,
                           preferred_element_type=F32)


def _rwkv_masks(is_bwd):
    rho = lax.broadcasted_iota(jnp.int32, (PAIR, PAIR), 0)
    sig = lax.broadcasted_iota(jnp.int32, (PAIR, PAIR), 1)
    same64 = (rho // 64) == (sig // 64)
    same32 = (rho // 32) == (sig // 32)
    same16 = (rho // 16) == (sig // 16)
    sgn = jnp.where(is_bwd, -1, 1).astype(jnp.int32)
    before = ((rho - sig) * sgn > 0) & same64
    eye = rho == sig
    t = lax.broadcasted_iota(jnp.int32, (RWKV_CHUNK, RWKV_CHUNK), 0)
    s = lax.broadcasted_iota(jnp.int32, (RWKV_CHUNK, RWKV_CHUNK), 1)
    tri = jnp.where((t - s) * sgn >= 0, 1.0, 0.0).astype(F32)
    return dict(same64=same64, d16=same16, off32=same32 & ~same16, off64=same64 & ~same32,
                before=before, before_eq=before | eye, eye=eye, tri=tri)


def _each(f, *lists):
    return [f(*xs) for xs in zip(*lists)]


def _rwkv_chunk(r, v, kk, ld, kd, bb, S, m):
    zero = jnp.zeros((), F32)
    bf = lambda z: z.astype(BF16)
    bd, before, eye = m["same64"], m["before"], m["eye"]
    before_eq2 = jnp.concatenate([m["before_eq"]] * 2, axis=1)
    stack2 = lambda x: jnp.where(bd, jnp.concatenate([x, x], axis=0), zero)

    cum = _each(lambda x: _mm(m["tri"], x, HI), ld)
    tot = _each(lambda x: jnp.sum(x, axis=0, keepdims=True), ld)
    e_inv = _each(lambda c: jnp.exp(-c), cum)
    e_end = _each(lambda c, t: jnp.exp(t - c), cum, tot)
    a2 = _each(lambda k, c, x: stack2(-k * jnp.exp(c - x)), kk, cum, ld)
    r2 = _each(lambda x, c: stack2(x * jnp.exp(c)), r, cum)
    b2 = _each(lambda x, e: stack2(x * e), bb, e_inv)
    k2 = _each(lambda x, e: stack2(x * e), kd, e_inv)
    bkg = _each(lambda x, y, e: bf(jnp.concatenate([stack2(x * e), stack2(y * e)], axis=0)),
                bb, kd, e_end)
    v2 = _each(stack2, v)
    sc = _each(lambda a, rr, b, k: _mm_nt(bf(jnp.concatenate([a, rr], axis=0)),
                                          bf(jnp.concatenate([b, k], axis=0))), a2, r2, b2, k2)
    mm = _each(lambda s: jnp.where(before, s[:PAIR, :PAIR], zero), sc)
    nn = _each(lambda s: bf(jnp.where(before, s[:PAIR, PAIR:], zero)), sc)
    pq = _each(lambda s: bf(jnp.where(before_eq2, s[PAIR:], zero)), sc)

    md = _each(lambda x: bf(jnp.where(m["d16"], x, zero)), mm)
    m2 = _each(lambda x: bf(_mm(x, x)), md)
    t_inv = _each(lambda x: jnp.where(eye, 1.0, zero) + x, md)
    tp = _each(lambda t, p: _mm(jnp.concatenate([bf(t), p], axis=0), p), t_inv, m2)
    t_inv = _each(lambda t, x: t + x[:PAIR], t_inv, tp)
    m4 = _each(lambda x: bf(x[PAIR:]), tp)
    tp = _each(lambda t, p: _mm(jnp.concatenate([bf(t), p], axis=0), p), t_inv, m4)
    t_inv = _each(lambda t, x: t + x[:PAIR], t_inv, tp)
    m8 = _each(lambda x: bf(x[PAIR:]), tp)
    t_inv = _each(lambda t, p: t + _mm(bf(t), p), t_inv, m8)
    for off in ("off32", "off64"):
        mo = _each(lambda x: bf(jnp.where(m[off], x, zero)), mm)
        tb = _each(bf, t_inv)
        mt = _each(lambda a, b: bf(_mm(a, b)), mo, tb)
        t_inv = _each(lambda t, a, b: t + _mm(a, b), t_inv, tb, mt)
    tb = _each(bf, t_inv)

    at = _each(lambda x: x.T, a2)
    vt = _each(lambda x: bf(x.T), v2)
    rt = _each(lambda x: x.T, r2)
    nvt = _each(_mm_nt, vt, nn)
    x1 = _each(lambda a, n, t: _mm_nt(bf(jnp.concatenate([a, n], axis=0)), t), at, nvt, tb)
    atp = _each(lambda x: bf(x[:PAIR]), x1)
    cv = _each(lambda x, y: jnp.concatenate([bf(x[PAIR:]), y], axis=1), x1, vt)
    g = _each(lambda a, b, t: _mm(a, b[:PAIR]) + jnp.where(eye, jnp.exp(t), zero), atp, bkg, tot)
    rtp = _each(lambda x, a, p: x + _mm_nt(a, p[:, :PAIR]), rt, atp, pq)
    c2 = _each(_mm, cv, bkg)
    c3 = _each(_mm_nt, cv, pq)
    so = _each(lambda s, a, b: _mm(bf(s), bf(jnp.concatenate([a, b], axis=1))), S, g, rtp)
    s_new = _each(lambda x, c: x[:, :PAIR] + c, so, c2)
    o2 = _each(lambda x, c: (x[:, PAIR:] + c).T, so, c3)
    return _each(lambda x: x[:RWKV_CHUNK] + x[RWKV_CHUNK:], o2), s_new


def _rwkv_kernel(r_ref, v_ref, kk_ref, ld_ref, kd_ref, bb_ref, o_ref, s_ref, *, npair, nch):
    is_bwd = pl.program_id(0) == 1
    blk = pl.program_id(3)

    @pl.when(blk == 0)
    def _():
        s_ref[...] = jnp.zeros_like(s_ref)

    m = _rwkv_masks(is_bwd)
    lanes = [slice(p * PAIR, (p + 1) * PAIR) for p in range(npair)]

    def chunk(c, carry):
        ci = jnp.where(is_bwd, nch - 1 - c, c)
        rows = pl.ds(pl.multiple_of(ci * RWKV_CHUNK, RWKV_CHUNK), RWKV_CHUNK)
        o, s_new = _rwkv_chunk(
            [r_ref[0, rows, ln] for ln in lanes], [v_ref[0, rows, ln] for ln in lanes],
            [kk_ref[0, rows, ln] for ln in lanes], [ld_ref[0, 0, rows, ln] for ln in lanes],
            [kd_ref[0, 0, rows, ln] for ln in lanes], [bb_ref[0, 0, rows, ln] for ln in lanes],
            [s_ref[p] for p in range(npair)], m)
        for p in range(npair):
            s_ref[p] = s_new[p]
            o_ref[0, 0, rows, lanes[p]] = o[p]
        return carry

    lax.fori_loop(0, nch, chunk, 0)


def rwkv_scan(r, v, kk, ld, kd, bb, *, ctx_len, tb=256, lanes=2048, interpret=False):
    b, tt, c = r.shape
    assert ctx_len % tb == 0 and tt % tb == 0 and c % lanes == 0 and lanes % PAIR == 0
    nctx = ctx_len // tb
    nblk = tt // tb

    def tmap(d, k):
        fwd = k
        bwd = jnp.where(k < nctx, nctx - 1 - k, nblk - 1 - (k - nctx))
        return jnp.where(d == 1, bwd, fwd)

    shared = pl.BlockSpec((1, tb, lanes), lambda d, bi, h, k: (bi, tmap(d, k), h))
    direc = pl.BlockSpec((1, 1, tb, lanes), lambda d, bi, h, k: (d, bi, tmap(d, k), h))
    out = pl.BlockSpec((1, 1, tb, lanes),
                       lambda d, bi, h, k: (d, bi, tmap(d, jnp.maximum(k, nctx)) - nctx, h))
    npair = lanes // PAIR
    return pl.pallas_call(
        functools.partial(_rwkv_kernel, npair=npair, nch=tb // RWKV_CHUNK),
        grid=(2, b, c // lanes, nblk),
        in_specs=[shared, shared, shared, direc, direc, direc],
        out_specs=out,
        out_shape=jax.ShapeDtypeStruct((2, b, tt - ctx_len, c), F32),
        scratch_shapes=[pltpu.VMEM((npair, PAIR, PAIR), F32)],
        compiler_params=pltpu.CompilerParams(
            dimension_semantics=("arbitrary", "arbitrary", "arbitrary", "arbitrary"),
            vmem_limit_bytes=VMEM_LIMIT),
        name="rwkv_scan",
        interpret=interpret,
    )(r, v, kk, ld, kd, bb)


def _time_block_map(nctx, nblk):
    def tmap(d, k):
        bwd = jnp.where(k < nctx, nctx - 1 - k, nblk - 1 - (k - nctx))
        return jnp.where(d == 1, bwd, k)
    return tmap


SSD_P = 64


def _ssd_kernel(xs_ref, bm_ref, cm_ref, dtc_ref, lac_ref, dtr_ref, lar_ref, y_ref, st_ref, *, chunk):
    is_bwd = pl.program_id(0) == 1
    blk = pl.program_id(3)

    @pl.when(blk == 0)
    def _():
        st_ref[...] = jnp.zeros_like(st_ref)

    sgn = jnp.where(is_bwd, -1, 1).astype(jnp.int32)
    l_i = lax.broadcasted_iota(jnp.int32, (chunk, chunk), 0)
    s_i = lax.broadcasted_iota(jnp.int32, (chunk, chunk), 1)
    incl = (l_i - s_i) * sgn >= 0
    tri = jnp.where(incl, 1.0, 0.0).astype(F32)
    lane = lax.broadcasted_iota(jnp.int32, (chunk, PAIR), 1)
    first = lane < SSD_P
    first_row = first[:1]

    ngb = dtc_ref.shape[2]
    nh = dtc_ref.shape[-1]
    nst = bm_ref.shape[-1] // ngb
    groups = list(range(ngb))
    pairs = [(gi, q) for gi in groups for q in range(nh // 2)]
    heads = [(gi, h) for gi in groups for h in range(nh)]
    pair_lanes = {(gi, q): slice((gi * nh // 2 + q) * PAIR, (gi * nh // 2 + q + 1) * PAIR)
                  for gi, q in pairs}

    bmat = [bm_ref[0, :, gi * nst:(gi + 1) * nst] for gi in groups]
    cmat_b = [cm_ref[0, :, gi * nst:(gi + 1) * nst].astype(BF16) for gi in groups]
    dtc = [dtc_ref[0, 0, gi] for gi in groups]
    lac = [lac_ref[0, 0, gi] for gi in groups]
    dtr = [dtr_ref[0, 0, gi] for gi in groups]
    lar = [lar_ref[0, 0, gi] for gi in groups]

    cum_col = _each(lambda z: _mm(tri, z, HI), lac)
    cum_row = _each(lambda z: _mm_nt(z, tri, HI), lar)
    cb = _each(lambda c, b: _mm_nt(c, b.astype(BF16)), cmat_b, bmat)
    b_t = _each(lambda b: b.T.astype(BF16), bmat)
    tot = _each(lambda z: jnp.sum(z, axis=0, keepdims=True), lac)
    coef = _each(lambda t, c, z: jnp.exp(t - c) * z, tot, cum_col, dtc)
    ecum = _each(jnp.exp, cum_col)
    etot = _each(jnp.exp, tot)

    xp = [xs_ref[0, :, pair_lanes[p]] for p in pairs]
    xp_b = _each(lambda z: z.astype(BF16), xp)
    st = [st_ref[:, pair_lanes[p]] for p in pairs]
    wmat = [(cb[gi] * jnp.exp(jnp.where(incl, cum_col[gi][:, h:h + 1] - cum_row[gi][h:h + 1, :],
                                        -jnp.inf)) * dtr[gi][h:h + 1, :]).astype(BF16)
            for gi, h in heads]
    yh = [_mm(w, xp_b[i // 2]) for i, w in enumerate(wmat)]
    yoff = [_mm(cmat_b[gi], s.astype(BF16)) for (gi, q), s in zip(pairs, st)]
    snew = [_mm(b_t[gi], (x * jnp.where(first, coef[gi][:, 2 * q:2 * q + 1],
                                        coef[gi][:, 2 * q + 1:2 * q + 2])).astype(BF16))
            for (gi, q), x in zip(pairs, xp)]
    for i, (gi, q) in enumerate(pairs):
        h0, h1 = 2 * q, 2 * q + 1
        y = jnp.where(first, yh[2 * i], yh[2 * i + 1])
        y = y + yoff[i] * jnp.where(first, ecum[gi][:, h0:h0 + 1], ecum[gi][:, h1:h1 + 1])
        st_ref[:, pair_lanes[(gi, q)]] = (
            st[i] * jnp.where(first_row, etot[gi][:, h0:h0 + 1], etot[gi][:, h1:h1 + 1]) + snew[i])
        y_ref[0, 0, :, pair_lanes[(gi, q)]] = y


def ssd_scan(xbc, dtc, lac, dtr, lar, *, ctx_len, nst, chunk=256, ngb=4, interpret=False):
    b, tt, _ = xbc.shape
    ng, nh = dtc.shape[2], dtc.shape[4]
    gw = ngb * nh * SSD_P
    width = ng * nh * SSD_P
    bw = ngb * nst
    assert xbc.shape[2] == width + 2 * ng * nst and width % bw == 0 and ng % ngb == 0
    assert ctx_len % chunk == 0 and tt % chunk == 0 and nh % 2 == 0
    nctx, nblk = ctx_len // chunk, tt // chunk
    tmap = _time_block_map(nctx, nblk)
    b0 = width // bw
    c0 = b0 + ng // ngb
    return pl.pallas_call(
        functools.partial(_ssd_kernel, chunk=chunk),
        grid=(2, b, ng // ngb, nblk),
        in_specs=[
            pl.BlockSpec((1, chunk, gw), lambda d, bi, g, k: (bi, tmap(d, k), g)),
            pl.BlockSpec((1, chunk, bw), lambda d, bi, g, k: (bi, tmap(d, k), b0 + g)),
            pl.BlockSpec((1, chunk, bw), lambda d, bi, g, k: (bi, tmap(d, k), c0 + g)),
            pl.BlockSpec((1, 1, ngb, chunk, nh), lambda d, bi, g, k: (d, bi, g, tmap(d, k), 0)),
            pl.BlockSpec((1, 1, ngb, chunk, nh), lambda d, bi, g, k: (d, bi, g, tmap(d, k), 0)),
            pl.BlockSpec((1, 1, ngb, nh, chunk), lambda d, bi, g, k: (d, bi, g, 0, tmap(d, k))),
            pl.BlockSpec((1, 1, ngb, nh, chunk), lambda d, bi, g, k: (d, bi, g, 0, tmap(d, k))),
        ],
        out_specs=pl.BlockSpec(
            (1, 1, chunk, gw),
            lambda d, bi, g, k: (d, bi, tmap(d, jnp.maximum(k, nctx)) - nctx, g)),
        out_shape=jax.ShapeDtypeStruct((2, b, tt - ctx_len, width), F32),
        scratch_shapes=[pltpu.VMEM((nst, gw), F32)],
        compiler_params=pltpu.CompilerParams(
            dimension_semantics=("arbitrary", "arbitrary", "arbitrary", "arbitrary")),
        name="ssd_scan",
        interpret=interpret,
    )(xbc, xbc, xbc, dtc, lac, dtr, lar)


RMS_EPS = 1e-6
VMEM_LIMIT = 56 * 1024 * 1024


def _mod_kernel(s_ref, w_ref, b_ref, o_ref):
    s = s_ref[...]
    s = s * jax.nn.sigmoid(s)
    o_ref[...] = _mm(s, w_ref[...], HI) + b_ref[...]


def mod_matmul(s, w, bias, *, tn=512, interpret=False):
    m, k = s.shape
    n = w.shape[1]
    return pl.pallas_call(
        _mod_kernel,
        grid=(n // tn,),
        in_specs=[pl.BlockSpec((m, k), lambda j: (0, 0)),
                  pl.BlockSpec((k, tn), lambda j: (0, j)),
                  pl.BlockSpec((1, tn), lambda j: (0, j))],
        out_specs=pl.BlockSpec((m, tn), lambda j: (0, j)),
        out_shape=jax.ShapeDtypeStruct((m, n), F32),
        compiler_params=pltpu.CompilerParams(dimension_semantics=("arbitrary",),
                                             vmem_limit_bytes=VMEM_LIMIT),
        name="mod_matmul",
        interpret=interpret,
    )(s, w, bias.reshape(1, n))


def _norm_modulate_kernel(a_ref, b_ref, g_ref, sh_ref, sc_ref, o_ref, *, n_first):
    def emit(h_ref):
        h = h_ref[...]
        hn = h * lax.rsqrt(jnp.mean(h * h, axis=-1, keepdims=True) + RMS_EPS)
        o_ref[...] = ((hn * g_ref[...]) * (1.0 + sc_ref[0]) + sh_ref[0]).astype(o_ref.dtype)

    is_first = pl.program_id(0) < n_first
    pl.when(is_first)(lambda: emit(a_ref))
    pl.when(jnp.logical_not(is_first))(lambda: emit(b_ref))


def norm_modulate(first, second, g, mod3, *, row_of_block, shift_idx, scale_idx, n_mod, tm=256,
                  interpret=False):
    d = second.shape[1]
    n_first = 0 if first is None else first.shape[0] // tm
    a = second if first is None else first
    rows = n_first * tm + second.shape[0]
    assert second.shape[0] % tm == 0 and a.shape[0] % tm == 0
    last_a = a.shape[0] // tm - 1
    return pl.pallas_call(
        functools.partial(_norm_modulate_kernel, n_first=n_first),
        grid=(rows // tm,),
        in_specs=[
            pl.BlockSpec((tm, d), lambda i: (jnp.minimum(i, last_a), 0)),
            pl.BlockSpec((tm, d), lambda i: (jnp.maximum(i - n_first, 0), 0)),
            pl.BlockSpec((1, d), lambda i: (0, 0)),
            pl.BlockSpec((1, 1, d), lambda i: (row_of_block(i) * n_mod + shift_idx, 0, 0)),
            pl.BlockSpec((1, 1, d), lambda i: (row_of_block(i) * n_mod + scale_idx, 0, 0)),
        ],
        out_specs=pl.BlockSpec((tm, d), lambda i: (i, 0)),
        out_shape=jax.ShapeDtypeStruct((rows, d), BF16),
        compiler_params=pltpu.CompilerParams(dimension_semantics=("arbitrary",),
                                             vmem_limit_bytes=VMEM_LIMIT),
        name="norm_modulate",
        interpret=interpret,
    )(a, second, g.reshape(1, d), mod3, mod3)


def _matmul_kernel(x_ref, w_ref, o_ref):
    o_ref[...] = _mm(x_ref[...], w_ref[...])


def matmul(x, w, *, tm, tn, interpret=False):
    rows, k = x.shape
    n = w.shape[1]
    assert rows % tm == 0 and n % tn == 0
    return pl.pallas_call(
        _matmul_kernel,
        grid=(rows // tm, n // tn),
        in_specs=[pl.BlockSpec((tm, k), lambda i, j: (i, 0)),
                  pl.BlockSpec((k, tn), lambda i, j: (0, j))],
        out_specs=pl.BlockSpec((tm, tn), lambda i, j: (i, j)),
        out_shape=jax.ShapeDtypeStruct((rows, n), F32),
        compiler_params=pltpu.CompilerParams(dimension_semantics=("arbitrary", "arbitrary"),
                                             vmem_limit_bytes=VMEM_LIMIT),
        name="matmul",
        interpret=interpret,
    )(x, w)


def _matmul_gated_resid_kernel(ya_ref, yb_ref, w_ref, x_ref, gt_ref, o_ref):
    ka = ya_ref.shape[1]
    acc = _mm(ya_ref[...], w_ref[:ka, :]) + _mm(yb_ref[...], w_ref[ka:, :])
    o_ref[...] = x_ref[...] + gt_ref[0] * acc


def matmul_gated_resid(ya, yb, w, resid, mod3, *, row_of_block, gate_idx, n_mod, tm, tn,
                       interpret=False):
    rows, ka = ya.shape
    kb = yb.shape[1]
    k, n = w.shape
    assert rows % tm == 0 and n % tn == 0 and ka + kb == k
    return pl.pallas_call(
        _matmul_gated_resid_kernel,
        grid=(rows // tm, n // tn),
        in_specs=[
            pl.BlockSpec((tm, ka), lambda i, j: (i, 0)),
            pl.BlockSpec((tm, kb), lambda i, j: (i, 0)),
            pl.BlockSpec((k, tn), lambda i, j: (0, j)),
            pl.BlockSpec((tm, tn), lambda i, j: (i, j)),
            pl.BlockSpec((1, 1, tn), lambda i, j: (row_of_block(i) * n_mod + gate_idx, 0, j)),
        ],
        out_specs=pl.BlockSpec((tm, tn), lambda i, j: (i, j)),
        out_shape=jax.ShapeDtypeStruct((rows, n), F32),
        compiler_params=pltpu.CompilerParams(dimension_semantics=("arbitrary", "arbitrary"),
                                             vmem_limit_bytes=VMEM_LIMIT),
        name="matmul_gated_resid",
        interpret=interpret,
    )(ya, yb, w, resid, mod3)


def _top_values(s, k):
    n = s.shape[0]
    row = lax.broadcasted_iota(jnp.int32, s.shape, 0)
    rank = jnp.full(s.shape, float(k), F32)
    vals = []
    for a in range(k):
        mx = jnp.max(s, axis=0, keepdims=True)
        first = jnp.min(jnp.where(s == mx, row, n), axis=0, keepdims=True)
        hit = row == first
        s = jnp.where(hit, -jnp.inf, s)
        rank = jnp.where(hit, float(a), rank)
        vals.append(mx)
    return jnp.concatenate(vals, axis=0), rank


def _peer_score_kernel(q_ref, k1_ref, k2_ref, cnt_ref, rank_ref, e1_ref, e2_ref, *, topk):
    nheads, nkeys, qd = k1_ref.shape
    for h in range(nheads):
        q1 = q_ref[:, h * 2 * qd:(h * 2 + 1) * qd]
        q2 = q_ref[:, (h * 2 + 1) * qd:(h * 2 + 2) * qd]
        s1 = _mm_nt(k1_ref[h], q1, HI)
        s2 = _mm_nt(k2_ref[h], q2, HI)
        v1, rank1 = _top_values(s1, topk)
        v2, rank2 = _top_values(s2, topk)
        cand = jnp.concatenate([v1[a:a + 1] + v2[:topk // (a + 1)] for a in range(topk)], axis=0)
        best, _ = _top_values(cand, topk)
        tau = best[topk - 1:topk]
        z = jnp.sum(jnp.exp(best - best[:1]), axis=0, keepdims=True)
        cnt = jnp.zeros_like(s1)
        for a in range(topk):
            cnt_a = jnp.sum(jnp.where(v1[a:a + 1] + v2 >= tau, 1.0, 0.0), axis=0, keepdims=True)
            cnt = jnp.where(rank1 == float(a), cnt_a, cnt)
        cnt_ref[h] = cnt.astype(cnt_ref.dtype)
        rank_ref[h] = rank2.astype(rank_ref.dtype)
        e1_ref[h] = jnp.exp(s1 - v1[:1]).astype(e1_ref.dtype)
        e2_ref[h] = (jnp.exp(s2 - v2[:1]) / z).astype(e2_ref.dtype)


def peer_scores(q, k1, k2, *, topk, tb=256, gate_dtype=BF16, interpret=False):
    t = q.shape[0]
    nheads, nkeys, qd = k1.shape
    big = pl.BlockSpec((nheads, nkeys, tb), lambda i: (0, 0, i))
    shape_of = lambda dt: jax.ShapeDtypeStruct((nheads, nkeys, t), dt)
    return pl.pallas_call(
        functools.partial(_peer_score_kernel, topk=topk),
        grid=(t // tb,),
        in_specs=[pl.BlockSpec((tb, q.shape[1]), lambda i: (i, 0)),
                  pl.BlockSpec(k1.shape, lambda i: (0, 0, 0)),
                  pl.BlockSpec(k2.shape, lambda i: (0, 0, 0))],
        out_specs=[big, big, big, big],
        out_shape=[shape_of(F32), shape_of(gate_dtype), shape_of(F32), shape_of(gate_dtype)],
        compiler_params=pltpu.CompilerParams(dimension_semantics=("arbitrary",),
                                             vmem_limit_bytes=VMEM_LIMIT),
        name="peer_scores",
        interpret=interpret,
    )(q, k1, k2)


def _peer_weight_kernel(u_ref, x_ref, cnt_ref, rank_ref, e1_ref, e2_ref, w_ref):
    nheads, nkeys, tb = rank_ref.shape
    gdt = rank_ref.dtype
    sub = 8 * 4 // jnp.dtype(gdt).itemsize
    a = _mm_nt(u_ref[...], x_ref[...])
    act = 0.5 * a * (1.0 + lax.erf(a * (1.0 / math.sqrt(2.0))))

    def rows_of(ref, h, ii):
        row = jnp.broadcast_to(ref[h, ii:ii + 1, :], (sub, tb)).astype(gdt)
        return jnp.concatenate([row] * (nkeys // sub), axis=0)

    for ii in range(u_ref.shape[0] // nkeys):
        gate = jnp.zeros((nkeys, tb), gdt)
        for h in range(nheads):
            sel = rank_ref[h] < rows_of(cnt_ref, h, ii)
            gate = gate + jnp.where(sel, e2_ref[h], jnp.zeros((), gdt)) * rows_of(e1_ref, h, ii)
        rows = slice(ii * nkeys, (ii + 1) * nkeys)
        w_ref[rows, :] = (act[rows].astype(gdt) * gate).astype(w_ref.dtype)


def peer_weights(u, xn, cnt, rank, e1, e2, *, te=1024, tb=512, interpret=False):
    ne, d = u.shape
    t = xn.shape[0]
    nheads, nkeys, _ = cnt.shape
    ni = te // nkeys
    s1_spec = pl.BlockSpec((nheads, ni, tb), lambda i, j: (0, j, i))
    s2_spec = pl.BlockSpec((nheads, nkeys, tb), lambda i, j: (0, 0, i))
    return pl.pallas_call(
        _peer_weight_kernel,
        grid=(t // tb, ne // te),
        in_specs=[pl.BlockSpec((te, d), lambda i, j: (j, 0)),
                  pl.BlockSpec((tb, d), lambda i, j: (i, 0)),
                  s1_spec, s2_spec, s1_spec, s2_spec],
        out_specs=pl.BlockSpec((te, tb), lambda i, j: (j, i)),
        out_shape=jax.ShapeDtypeStruct((ne, t), BF16),
        compiler_params=pltpu.CompilerParams(dimension_semantics=("arbitrary", "arbitrary"),
                                             vmem_limit_bytes=VMEM_LIMIT),
        name="peer_weights",
        interpret=interpret,
    )(u, xn, cnt, rank, e1, e2)


def _peer_out_kernel(w_ref, v_ref, o_ref):
    @pl.when(pl.program_id(1) == 0)
    def _():
        o_ref[...] = jnp.zeros_like(o_ref)

    o_ref[...] += lax.dot_general(w_ref[...], v_ref[...], (((0,), (0,)), ((), ())),
                                  preferred_element_type=F32)


def peer_out(wt, v, *, tb=512, te=1024, interpret=False):
    ne, t = wt.shape
    d = v.shape[1]
    return pl.pallas_call(
        _peer_out_kernel,
        grid=(t // tb, ne // te),
        in_specs=[pl.BlockSpec((te, tb), lambda i, j: (j, i)),
                  pl.BlockSpec((te, d), lambda i, j: (j, 0))],
        out_specs=pl.BlockSpec((tb, d), lambda i, j: (i, 0)),
        out_shape=jax.ShapeDtypeStruct((t, d), F32),
        compiler_params=pltpu.CompilerParams(dimension_semantics=("arbitrary", "arbitrary"),
                                             vmem_limit_bytes=VMEM_LIMIT),
        name="peer_out",
        interpret=interpret,
    )(wt, v)


def _final_kernel(h_ref, p_ref, gt_ref, g_ref, o_ref):
    h = h_ref[...] + gt_ref[0] * p_ref[...]
    o_ref[...] = h * lax.rsqrt(jnp.mean(h * h, axis=-1, keepdims=True) + RMS_EPS) * g_ref[...]


def final_norm(h, p, mod3, g, *, row_of_block, gate_idx, n_mod, tm=256, interpret=False):
    rows, d = h.shape
    blk = pl.BlockSpec((tm, d), lambda i: (i, 0))
    return pl.pallas_call(
        _final_kernel,
        grid=(rows // tm,),
        in_specs=[blk, blk,
                  pl.BlockSpec((1, 1, d), lambda i: (row_of_block(i) * n_mod + gate_idx, 0, 0)),
                  pl.BlockSpec((1, d), lambda i: (0, 0))],
        out_specs=blk,
        out_shape=jax.ShapeDtypeStruct((rows, d), F32),
        compiler_params=pltpu.CompilerParams(dimension_semantics=("arbitrary",),
                                             vmem_limit_bytes=VMEM_LIMIT),
        name="final_norm",
        interpret=interpret,
    )(h, p, mod3, g.reshape(1, d))


N_MOD = 6
GRID_W = 64
RWKV_GN_EPS = 64e-5
SSM_STATE = 128
PEER_TOPK = 16


LANES = 128
TOK_BLK = 256


def _block_ones(group):
    i = lax.broadcasted_iota(jnp.int32, (LANES, LANES), 0) // group
    j = lax.broadcasted_iota(jnp.int32, (LANES, LANES), 1) // group
    return jnp.where(i == j, 1.0, 0.0).astype(BF16)


def _ones_mm(x, ones):
    hi = x.astype(BF16)
    lo = (x - hi.astype(F32)).astype(BF16)
    return _mm(hi, ones) + _mm(lo, ones)


def _segsum(x, ones):
    return jnp.concatenate([_ones_mm(x[:, i * LANES:(i + 1) * LANES], ones)
                            for i in range(x.shape[1] // LANES)], axis=1)


def _softplus(x):
    return jnp.maximum(x, 0.0) + jnp.log(1.0 + jnp.exp(-jnp.abs(x)))


def _shift_mix(cur_ref, prev_ref, next_ref, mu_ref, is_ctx, top, bottom, grid_w):
    cur = cur_ref[...]
    n, w = cur.shape
    row = lax.broadcasted_iota(jnp.int32, (n, w), 0)
    lane = lax.broadcasted_iota(jnp.int32, (n, w), 1)
    before = pltpu.roll(cur, 1, axis=0)
    after = pltpu.roll(cur, n - 1, axis=0)
    col = row % grid_w
    left = jnp.where(col == 0, 0.0, before)
    right = jnp.where(col == grid_w - 1, 0.0, after)
    up = jnp.concatenate([prev_ref[...] * (1.0 - top), cur[:n - grid_w]], axis=0)
    down = jnp.concatenate([cur[grid_w:], next_ref[...] * (1.0 - bottom)], axis=0)
    slot = lane % 4
    sh_grid = jnp.where(slot == 0, left, jnp.where(slot == 1, right, jnp.where(slot == 2, up, down)))
    sh_seq = jnp.where(lane % 2 == 0, jnp.where(row == 0, 0.0, before),
                       jnp.where(row == n - 1, 0.0, after))
    shifted = jnp.where(is_ctx, sh_seq, sh_grid)
    return cur + mu_ref[...] * (shifted - cur)


def _rwkv_prep_kernel(r_c, r_p, r_n, k_c, k_p, k_n, v_c, v_p, v_n, l_c, l_p, l_n,
                      mu_r, mu_k, mu_v, mu_l, w2_ref, a2_ref, g2_ref, w0_ref, a0_ref,
                      kk_ref, ka_ref, rk_ref,
                      r_o, v_o, kkn_o, ld_o, kd_o, bb_o, bonus_o, gate_o,
                      *, nctx_blk, blk_per_seq, grid_w):
    i = pl.program_id(0)
    is_ctx = i < nctx_blk
    tblk = (i - nctx_blk) % blk_per_seq
    top = jnp.where(tblk == 0, 1.0, 0.0)
    bottom = jnp.where(tblk == blk_per_seq - 1, 1.0, 0.0)
    mix = functools.partial(_shift_mix, is_ctx=is_ctx, top=top, bottom=bottom, grid_w=grid_w)
    r = mix(r_c, r_p, r_n, mu_r)
    k = mix(k_c, k_p, k_n, mu_k)
    v = mix(v_c, v_p, v_n, mu_v)
    low = mix(l_c, l_p, l_n, mu_l)
    cw = r.shape[1]

    w_low = _mm(jnp.tanh(low).astype(BF16), w2_ref[...])
    a_low = _mm(low.astype(BF16), a2_ref[...])
    gate_o[0] = _mm(jax.nn.sigmoid(low).astype(BF16), g2_ref[...])

    ones = _block_ones(RWKV_HEAD)
    kraw = k * kk_ref[...]
    kkn = kraw * lax.rsqrt(jnp.maximum(_segsum(kraw * kraw, ones), 1e-24))
    ksum = jnp.zeros_like(k)
    for d in range(2):
        cols = slice(d * cw, (d + 1) * cw)
        w = -_softplus(-(w0_ref[:, cols] + w_low[:, cols])) - 0.5
        a = jax.nn.sigmoid(a0_ref[:, cols] + a_low[:, cols])
        k_dir = k * (1.0 + (a - 1.0) * ka_ref[...])
        ld_o[d, 0] = -jnp.exp(w)
        kd_o[d, 0] = k_dir
        bb_o[d, 0] = kkn * a
        ksum = ksum + k_dir
    r_o[0] = r
    v_o[0] = v
    kkn_o[0] = kkn
    bonus_o[0] = _segsum(r * ksum * rk_ref[...], ones) * v


def rwkv_prep(proj, mu_p, w2g, a2g, g2g, w0g, a0g, k_k, k_a, r_k, *, nb, cl, t, cw, lr_off, lr_w,
              grid_w, cg=512, interpret=False):
    tb = TOK_BLK
    assert cl == tb and t % tb == 0 and tb % grid_w == 0 and cw % cg == 0 and lr_off % lr_w == 0
    nctx_blk = nb
    blk_per_seq = t // tb
    nblk = nctx_blk + nb * blk_per_seq
    hb = tb // grid_w
    last_h = nblk * hb - 1
    ncg = cw // cg

    def seg(col0, width):
        cb = col0 // width
        return [pl.BlockSpec((tb, width), lambda i, g, cb=cb, s=(width == cg): (i, cb + g * s)),
                pl.BlockSpec((grid_w, width),
                             lambda i, g, cb=cb, s=(width == cg): (jnp.maximum(i * hb - 1, 0), cb + g * s)),
                pl.BlockSpec((grid_w, width),
                             lambda i, g, cb=cb, s=(width == cg): (jnp.minimum((i + 1) * hb, last_h), cb + g * s))]

    def mu_spec(col0, width):
        cb = col0 // width
        return pl.BlockSpec((1, width), lambda i, g, cb=cb, s=(width == cg): (0, cb + g * s))

    def tok(i):
        b = jnp.where(i < nctx_blk, i, (i - nctx_blk) // blk_per_seq)
        tk = jnp.where(i < nctx_blk, 0, 1 + (i - nctx_blk) % blk_per_seq)
        return b, tk

    def out1(i, g):
        b, tk = tok(i)
        return (b, tk, g)

    def out2(i, g):
        b, tk = tok(i)
        return (0, b, tk, g)

    per_c = pl.BlockSpec((1, cg), lambda i, g: (0, g))
    one_shape = jax.ShapeDtypeStruct((nb, cl + t, cw), F32)
    two_shape = jax.ShapeDtypeStruct((2, nb, cl + t, cw), F32)
    one_spec = pl.BlockSpec((1, tb, cg), out1)
    two_spec = pl.BlockSpec((2, 1, tb, cg), out2)
    return pl.pallas_call(
        functools.partial(_rwkv_prep_kernel, nctx_blk=nctx_blk, blk_per_seq=blk_per_seq,
                          grid_w=grid_w),
        grid=(nblk, ncg),
        in_specs=(seg(0, cg) + seg(cw, cg) + seg(2 * cw, cg) + seg(lr_off, lr_w)
                  + [mu_spec(0, cg), mu_spec(cw, cg), mu_spec(2 * cw, cg), mu_spec(lr_off, lr_w)]
                  + [pl.BlockSpec((lr_w, 2 * cg), lambda i, g: (0, g)),
                     pl.BlockSpec((lr_w, 2 * cg), lambda i, g: (0, g)),
                     pl.BlockSpec((lr_w, cg), lambda i, g: (0, g)),
                     pl.BlockSpec((1, 2 * cg), lambda i, g: (0, g)),
                     pl.BlockSpec((1, 2 * cg), lambda i, g: (0, g)),
                     per_c, per_c, per_c]),
        out_specs=[one_spec, one_spec, one_spec, two_spec, two_spec, two_spec, one_spec, one_spec],
        out_shape=[one_shape] * 3 + [two_shape] * 3 + [one_shape] * 2,
        compiler_params=pltpu.CompilerParams(dimension_semantics=("arbitrary", "arbitrary"),
                                             vmem_limit_bytes=VMEM_LIMIT),
        name="rwkv_prep",
        interpret=interpret,
    )(*([proj] * 12), mu_p, mu_p, mu_p, mu_p, w2g, a2g, g2g, w0g, a0g,
      k_k.reshape(1, cw), k_a.reshape(1, cw), r_k.reshape(1, cw))


def _ssd_prep_kernel(c_ref, p_ref, n_ref, w_ref, b_ref, o_ref, *, nctx_blk, blk_per_seq):
    i = pl.program_id(0)
    is_ctx = i < nctx_blk
    tblk = (i - nctx_blk) % blk_per_seq
    first = jnp.logical_or(is_ctx, tblk == 0)
    last = jnp.logical_or(is_ctx, tblk == blk_per_seq - 1)
    cur = c_ref[...]
    n = cur.shape[0]
    row = lax.broadcasted_iota(jnp.int32, cur.shape, 0)
    halo = p_ref.shape[0]
    prev_row = p_ref[halo - 1:halo, :] * jnp.where(first, 0.0, 1.0)
    next_row = n_ref[0:1, :] * jnp.where(last, 0.0, 1.0)
    before = jnp.where(row == 0, prev_row, pltpu.roll(cur, 1, axis=0))
    after = jnp.where(row == n - 1, next_row, pltpu.roll(cur, n - 1, axis=0))
    y = w_ref[0:1, :] * before + w_ref[1:2, :] * cur + w_ref[2:3, :] * after + b_ref[...]
    o_ref[0] = y * jax.nn.sigmoid(y)


def ssd_prep(proj, conv_w, conv_b, *, nb, cl, t, col0, cg=512, interpret=False):
    tb = TOK_BLK
    halo = 8
    xbc_w = conv_w.shape[1]
    assert conv_w.shape[0] == 3 and cl == tb and t % tb == 0 and col0 % cg == 0 and xbc_w % cg == 0
    nctx_blk = nb
    blk_per_seq = t // tb
    nblk = nctx_blk + nb * blk_per_seq
    hb = tb // halo
    last_h = nblk * hb - 1
    cb = col0 // cg

    def out_map(i, g):
        b = jnp.where(i < nctx_blk, i, (i - nctx_blk) // blk_per_seq)
        tk = jnp.where(i < nctx_blk, 0, 1 + (i - nctx_blk) % blk_per_seq)
        return (b, tk, g)

    return pl.pallas_call(
        functools.partial(_ssd_prep_kernel, nctx_blk=nctx_blk, blk_per_seq=blk_per_seq),
        grid=(nblk, xbc_w // cg),
        in_specs=[pl.BlockSpec((tb, cg), lambda i, g: (i, cb + g)),
                  pl.BlockSpec((halo, cg), lambda i, g: (jnp.maximum(i * hb - 1, 0), cb + g)),
                  pl.BlockSpec((halo, cg), lambda i, g: (jnp.minimum((i + 1) * hb, last_h), cb + g)),
                  pl.BlockSpec((3, cg), lambda i, g: (0, g)),
                  pl.BlockSpec((1, cg), lambda i, g: (0, g))],
        out_specs=pl.BlockSpec((1, tb, cg), out_map),
        out_shape=jax.ShapeDtypeStruct((nb, cl + t, xbc_w), F32),
        compiler_params=pltpu.CompilerParams(dimension_semantics=("arbitrary", "arbitrary"),
                                             vmem_limit_bytes=VMEM_LIMIT),
        name="ssd_prep",
        interpret=interpret,
    )(proj, proj, proj, conv_w, conv_b.reshape(1, xbc_w))


def _mixer_finish_kernel(o_ref, bonus_ref, gate_ref, lnw_ref, lnb_ref,
                         ys_ref, xs_ref, z_ref, dsk_ref, nw_ref, yr_ref, ysd_ref, *, ssd_group):
    o = o_ref[0, 0] + o_ref[1, 0]
    ones = _block_ones(RWKV_HEAD)
    mean = _segsum(o, ones) * (1.0 / RWKV_HEAD)
    cen = o - mean
    var = _segsum(cen * cen, ones) * (1.0 / RWKV_HEAD)
    on = cen * lax.rsqrt(var + RWKV_GN_EPS)
    yr_ref[...] = ((on * lnw_ref[...] + lnb_ref[...] + bonus_ref[0]) * gate_ref[0]).astype(yr_ref.dtype)

    z = z_ref[...]
    y = (ys_ref[0, 0] + ys_ref[1, 0] + dsk_ref[...] * xs_ref[0]) * (z * jax.nn.sigmoid(z))
    sq = y * y
    all_ones = jnp.ones((LANES, LANES), BF16)
    tiles = ssd_group // LANES
    parts = []
    for gi in range(y.shape[1] // ssd_group):
        acc = sq[:, gi * ssd_group:gi * ssd_group + LANES]
        for ti in range(1, tiles):
            acc = acc + sq[:, gi * ssd_group + ti * LANES:gi * ssd_group + (ti + 1) * LANES]
        ms = _ones_mm(acc, all_ones) * (1.0 / ssd_group)
        parts.extend([ms] * tiles)
    ms = jnp.concatenate(parts, axis=1)
    ysd_ref[...] = (y * lax.rsqrt(ms + RMS_EPS) * nw_ref[...]).astype(ysd_ref.dtype)


def mixer_finish(o, bonus, gate, ln_w, ln_b, ys, xbc, proj, d_skip, norm_w, *, nb, cl, t, z_col0,
                 ssd_group, cg=512, interpret=False):
    tb = TOK_BLK
    cw = o.shape[-1]
    assert cl % tb == 0 and t % tb == 0 and z_col0 % cg == 0 and cg % ssd_group == 0
    off = cl // tb
    bps = t // tb
    zrow0 = nb * cl // tb
    zc = z_col0 // cg
    two = pl.BlockSpec((2, 1, tb, cg), lambda b, k, g: (0, b, k, g))
    shifted = pl.BlockSpec((1, tb, cg), lambda b, k, g: (b, k + off, g))
    per_c = pl.BlockSpec((1, cg), lambda b, k, g: (0, g))
    out = pl.BlockSpec((tb, cg), lambda b, k, g: (b * bps + k, g))
    return pl.pallas_call(
        functools.partial(_mixer_finish_kernel, ssd_group=ssd_group),
        grid=(nb, bps, cw // cg),
        in_specs=[two, shifted, shifted, per_c, per_c, two, shifted,
                  pl.BlockSpec((tb, cg), lambda b, k, g: (zrow0 + b * bps + k, zc + g)),
                  per_c, per_c],
        out_specs=[out, out],
        out_shape=[jax.ShapeDtypeStruct((nb * t, cw), BF16)] * 2,
        compiler_params=pltpu.CompilerParams(
            dimension_semantics=("arbitrary", "arbitrary", "arbitrary"),
            vmem_limit_bytes=VMEM_LIMIT),
        name="mixer_finish",
        interpret=interpret,
    )(o, bonus, gate, ln_w.reshape(1, cw), ln_b.reshape(1, cw), ys, xbc, proj,
      d_skip.reshape(1, cw), norm_w.reshape(1, cw))


def _col_form(zz, ng):
    b, tt, _, nh = zz.shape
    return jnp.transpose(zz.reshape(b, tt, 2, ng, nh // ng), (2, 0, 3, 1, 4))


def _row_form(zz, ng):
    b, tt, _, nh = zz.shape
    return jnp.transpose(zz.reshape(b, tt, 2, ng, nh // ng), (2, 0, 3, 4, 1))


COL_BLK = 512
PROJ_ROWS = 1024
LR_BLK = 768


def _by_group(z, ncg):
    rows, _, c = z.shape
    return jnp.transpose(z.reshape(rows, 2, ncg, c // ncg), (0, 2, 1, 3)).reshape(rows, 2 * c)


def _in_proj_layout(w_in, rwkv_mu, w2, a2, g2, w0, a0, cw, sw, xbc_w, nh):
    d = w_in.shape[0]
    rc = rwkv_mu.shape[0]
    lr = rc - 3 * cw
    dr, ir, gr = w2.shape[1], a2.shape[1], g2.shape[0]
    assert lr == 2 * dr + 2 * ir + gr and lr <= LR_BLK and (3 * cw) % LR_BLK == 0
    lr_off = 3 * cw
    z_off = -(-(lr_off + LR_BLK) // COL_BLK) * COL_BLK
    xbc_off = z_off + sw
    dt_off = xbc_off + xbc_w
    total = -(-(dt_off + 2 * nh) // COL_BLK) * COL_BLK
    zc = lambda n: jnp.zeros((d, n), w_in.dtype)
    w_p = jnp.concatenate([
        w_in[:, :rc], zc(z_off - rc), w_in[:, rc:rc + sw + xbc_w + 2 * nh],
        zc(total - dt_off - 2 * nh)], axis=1).astype(BF16)
    mu_w = -(-(lr_off + LR_BLK) // (3 * COL_BLK)) * 3 * COL_BLK
    mu_p = jnp.pad(rwkv_mu, (0, mu_w - rc)).reshape(1, mu_w)
    ncg = cw // COL_BLK
    wd = jnp.zeros((LR_BLK, 2, cw), F32).at[0:dr, 0].set(w2[0]).at[dr:2 * dr, 1].set(w2[1])
    ad = (jnp.zeros((LR_BLK, 2, cw), F32).at[2 * dr:2 * dr + ir, 0].set(a2[0])
          .at[2 * dr + ir:2 * dr + 2 * ir, 1].set(a2[1]))
    gd = jnp.zeros((LR_BLK, cw), F32).at[2 * dr + 2 * ir:lr].set(g2)
    return dict(w=w_p, mu=mu_p, lr_off=lr_off, z_off=z_off, xbc_off=xbc_off, dt_off=dt_off,
                w2g=_by_group(wd, ncg).astype(BF16), a2g=_by_group(ad, ncg).astype(BF16),
                g2g=gd.astype(BF16), w0g=_by_group(w0[None], ncg), a0g=_by_group(a0[None], ncg))


def kernel(x, c, ctx, c_ctx, w_mod, b_mod, norm1_g, norm2_g, w_in, w_out, rwkv_mu, rwkv_w0, rwkv_w2, rwkv_a0, rwkv_a2, rwkv_g2, rwkv_k_k, rwkv_k_a, rwkv_r_k, rwkv_ln_w, rwkv_ln_b, ssm_conv_w, ssm_conv_b, ssm_dt_bias, ssm_a_log, ssm_d, ssm_norm_w, peer_wq, peer_k1, peer_k2, peer_u, peer_v, final_g):
    assert w_in.shape[0] == 1, "single-layer block"
    nb, t, d = x.shape
    cl = ctx.shape[1]
    mod_rows = 8
    s = jnp.concatenate([c, c_ctx[None], jnp.zeros((mod_rows - nb - 1, d), F32)], axis=0)
    mod3 = mod_matmul(s, w_mod[0], b_mod[0]).reshape(mod_rows * N_MOD, 1, d)

    tm = 512
    x_blk = t // tm
    tn_blk = TOK_BLK
    nctx_blk = nb * cl // tn_blk
    xn1 = norm_modulate(
        ctx.reshape(nb * cl, d), x.reshape(nb * t, d), norm1_g[0], mod3,
        row_of_block=lambda i: jnp.where(i < nctx_blk, nb, (i - nctx_blk) // (t // tn_blk)),
        shift_idx=0, scale_idx=1, n_mod=N_MOD, tm=tn_blk)
    cw = rwkv_k_k.shape[1]
    sw = ssm_norm_w.shape[1]
    xbc_w = ssm_conv_w.shape[2]
    nh = ssm_a_log.shape[2]
    ng = (xbc_w - sw) // (2 * SSM_STATE)
    lay = _in_proj_layout(w_in[0], rwkv_mu[0], rwkv_w2[0], rwkv_a2[0], rwkv_g2[0], rwkv_w0[0],
                          rwkv_a0[0], cw, sw, xbc_w, nh)
    proj = matmul(xn1, lay["w"], tm=PROJ_ROWS, tn=COL_BLK)

    r, v, kk, ld, kd, bb, bonus, gate = rwkv_prep(
        proj, lay["mu"], lay["w2g"], lay["a2g"], lay["g2g"], lay["w0g"], lay["a0g"],
        rwkv_k_k[0], rwkv_k_a[0], rwkv_r_k[0].reshape(cw), nb=nb, cl=cl, t=t, cw=cw,
        lr_off=lay["lr_off"], lr_w=LR_BLK, grid_w=GRID_W, cg=COL_BLK)
    o = rwkv_scan(r, v, kk, ld, kd, bb, ctx_len=cl)

    xbc = ssd_prep(proj, ssm_conv_w[0], ssm_conv_b[0], nb=nb, cl=cl, t=t, col0=lay["xbc_off"],
                   cg=2 * COL_BLK)
    dt_raw = proj[:, lay["dt_off"]:lay["dt_off"] + 2 * nh]
    dt_raw = jnp.concatenate([dt_raw[:nb * cl].reshape(nb, cl, 2, nh),
                              dt_raw[nb * cl:].reshape(nb, t, 2, nh)], axis=1)
    dt = jax.nn.softplus(dt_raw + ssm_dt_bias[0])
    la = dt * -jnp.exp(ssm_a_log[0])
    ys = ssd_scan(xbc, _col_form(dt, ng), _col_form(la, ng), _row_form(dt, ng), _row_form(la, ng),
                  ctx_len=cl, nst=SSM_STATE)

    y_rwkv, y_ssd = mixer_finish(
        o, bonus, gate, rwkv_ln_w[0], rwkv_ln_b[0], ys, xbc, proj,
        jnp.repeat(ssm_d[0], sw // nh), ssm_norm_w[0], nb=nb, cl=cl, t=t, z_col0=lay["z_off"],
        ssd_group=sw // ng, cg=COL_BLK)
    by_batch = lambda i: i // x_blk
    h1 = matmul_gated_resid(y_rwkv, y_ssd, w_out[0].astype(BF16), x.reshape(nb * t, d), mod3,
                            row_of_block=by_batch, gate_idx=2, n_mod=N_MOD, tm=tm, tn=1024)

    xn2 = norm_modulate(None, h1, norm2_g[0], mod3, row_of_block=lambda i: i // (t // tn_blk),
                        shift_idx=3, scale_idx=4, n_mod=N_MOD, tm=tn_blk)
    q = matmul(xn2, peer_wq[0].astype(BF16), tm=PROJ_ROWS, tn=COL_BLK)
    cnt, rank, e1, e2 = peer_scores(q, peer_k1[0], peer_k2[0], topk=PEER_TOPK)
    wt = peer_weights(peer_u[0].astype(BF16), xn2, cnt, rank, e1, e2)
    p = peer_out(wt, peer_v[0].astype(BF16))
    tf = 256
    out = final_norm(h1, p, mod3, final_g, row_of_block=lambda i: i // (t // tf), gate_idx=5,
                     n_mod=N_MOD, tm=tf)
    return out.reshape(nb, t, d)
```
